```python
import math
import jax, jax.numpy as jnp
from jax import lax
import numpy as np

D_MODEL = 2048
BATCH = 1
SEQ = 8192
DEPTH = 2
DEC_BATCH = 128
DEC_SEQ = 1
PAST_LEN = 2048
PAGE_SIZE = 128

HEAD_DIM = 128
N_EVEN = (DEPTH + 1) // 2
N_ODD = DEPTH // 2
NSA_H = D_MODEL // 256
NSA_KVH = NSA_H // 4
NSA_GROUP = NSA_H // NSA_KVH
CMP_BLOCK = 32
CMP_STRIDE = 16
SEL_BLOCK = 64
SEL_TOPK = 16
WINDOW = 512
SB_H = D_MODEL // 256
FOX_H = D_MODEL // HEAD_DIM
MEM_LEN = 256
MEM_H = 4
MEM_HD = 128
ROPE_THETA = 10000.0
Q_BLOCK = 128
EPS = 1e-6
NEG = -1e30

EVEN_SPLIT_SIZES = (NSA_H * HEAD_DIM, 6 * NSA_KVH * HEAD_DIM, 3 * NSA_H, NSA_H * HEAD_DIM,
                    3 * SB_H * HEAD_DIM, SB_H * HEAD_DIM)
EVEN_COLS = sum(EVEN_SPLIT_SIZES)
EVEN_MIX = (NSA_H + SB_H) * HEAD_DIM
ODD_COLS = 3 * FOX_H * HEAD_DIM + FOX_H + FOX_H * HEAD_DIM
ODD_MIX = FOX_H * HEAD_DIM

kernel_name = "nsa_stickbreak_fox_hybrid_step"


def rmsnorm(x, g):
    xf = x.astype(jnp.float32)
    r = lax.rsqrt(jnp.mean(xf * xf, axis=-1, keepdims=True) + EPS)
    return (xf * r).astype(x.dtype) * g


def rope(x, pos):
    half = x.shape[-1] // 2
    inv = ROPE_THETA ** (-jnp.arange(half, dtype=jnp.float32) / half)
    ang = pos.astype(jnp.float32)[:, None] * inv[None, :]
    shape = (pos.shape[0],) + (1,) * (x.ndim - 3) + (half,)
    cos = jnp.cos(ang).reshape(shape)
    sin = jnp.sin(ang).reshape(shape)
    xf = x.astype(jnp.float32)
    x1, x2 = xf[..., :half], xf[..., half:]
    return jnp.concatenate([x1 * cos - x2 * sin, x2 * cos + x1 * sin], axis=-1).astype(x.dtype)


def masked_softmax(s, mask):
    s = jnp.where(mask, s.astype(jnp.float32), NEG)
    m = jnp.max(s, axis=-1, keepdims=True)
    e = jnp.where(mask, jnp.exp(s - m), 0.0)
    return e / jnp.maximum(jnp.sum(e, axis=-1, keepdims=True), 1e-30)


def gather_pages(pool, page_table):
    rows = pool[page_table]
    return rows.reshape((rows.shape[0], rows.shape[1] * rows.shape[2]) + rows.shape[3:])


def block_sweep(fn, n_blocks, batch):
    out = lax.map(fn, jnp.arange(n_blocks))
    out = jnp.moveaxis(out, 0, 1)
    return out.reshape((batch, n_blocks * Q_BLOCK) + out.shape[3:])


def nsa_compress(rows, pe, w1, w2):
    B, L, G, d = rows.shape
    r = CMP_BLOCK // CMP_STRIDE
    chunks = rows.reshape(B, L // CMP_STRIDE, CMP_STRIDE, G, d)
    nc = L // CMP_STRIDE - r + 1
    blocks = jnp.concatenate([chunks[:, i:i + nc] for i in range(r)], axis=2)
    blocks = blocks + pe[None, None, :, None, :]
    flat = jnp.moveaxis(blocks, 3, 2).reshape(B, nc, G, CMP_BLOCK * d)
    h = jax.nn.silu(jnp.einsum('bngi,gio->bngo', flat, w1))
    return jnp.einsum('bngi,gio->bngo', h, w2)


def to_sel_blocks(rows):
    B, L, G, d = rows.shape
    return rows.reshape(B, L // SEL_BLOCK, SEL_BLOCK, G, d).transpose(0, 3, 1, 2, 4)


def nsa_attend(q, gates, qpos, kc, vc, sk, sv, kw, vw, wpos):
    B, Tq, G, R, d = q.shape
    scale = d ** -0.5
    nc = kc.shape[1]
    ns = sk.shape[2]
    c_end = jnp.arange(nc) * CMP_STRIDE + (CMP_BLOCK - 1)
    s_c = jnp.einsum('bqgrd,bngd->bgrqn', q, kc).astype(jnp.float32) * scale
    p_c = masked_softmax(s_c, c_end[None, :] <= qpos[:, None])
    o_c = jnp.einsum('bgrqn,bngd->bqgrd', p_c.astype(vc.dtype), vc)
    r = CMP_BLOCK // CMP_STRIDE
    m = SEL_BLOCK // CMP_STRIDE
    imp = jnp.sum(p_c, axis=2)
    imp = jnp.pad(imp, ((0, 0), (0, 0), (0, 0), (r - 1, m * ns - nc)))
    imp_sel = imp[..., 0::m][..., :ns]
    for o in range(1, m + r - 1):
        imp_sel = imp_sel + imp[..., o::m][..., :ns]
    blk = jnp.arange(ns)
    visible = (blk * SEL_BLOCK)[None, :] <= qpos[:, None]
    is_cur = blk[None, :] == (qpos // SEL_BLOCK)[:, None]
    score = jnp.where(is_cur, 1e9, jnp.where(visible, imp_sel, -1e9))
    top_score, idx = lax.top_k(score, min(SEL_TOPK, ns))
    sel_valid = top_score > -1e8
    b_i = jnp.arange(B)[:, None, None, None]
    g_i = jnp.arange(G)[None, :, None, None]
    ksel = sk[b_i, g_i, idx]
    vsel = sv[b_i, g_i, idx]
    kt = idx.shape[-1]
    kpos = idx[..., None] * SEL_BLOCK + jnp.arange(SEL_BLOCK)
    mask_s = sel_valid[..., None] & (kpos <= qpos[None, None, :, None, None])
    s_s = jnp.einsum('bqgrd,bgqnsd->bgrqns', q, ksel).astype(jnp.float32) * scale
    p_s = masked_softmax(s_s.reshape(B, G, R, Tq, kt * SEL_BLOCK),
                         mask_s.reshape(B, G, 1, Tq, kt * SEL_BLOCK))
    o_s = jnp.einsum('bgrqk,bgqkd->bqgrd', p_s.astype(vsel.dtype), vsel.reshape(B, G, Tq, kt * SEL_BLOCK, d))
    mask_w = (wpos[None, :] <= qpos[:, None]) & (wpos[None, :] > qpos[:, None] - WINDOW) & (wpos[None, :] >= 0)
    s_w = jnp.einsum('bqgrd,bkgd->bgrqk', q, kw).astype(jnp.float32) * scale
    p_w = masked_softmax(s_w, mask_w)
    o_w = jnp.einsum('bgrqk,bkgd->bqgrd', p_w.astype(vw.dtype), vw)
    return gates[..., 0:1] * o_c + gates[..., 1:2] * o_s + gates[..., 2:3] * o_w


def stick_breaking(q, k, v, qpos, kpos):
    z = jnp.einsum('bqhd,bkhd->bhqk', q, k).astype(jnp.float32) * (q.shape[-1] ** -0.5)
    mask = kpos[None, :] < qpos[:, None]
    log_keep = jnp.where(mask, -jax.nn.softplus(z), 0.0)
    after = lax.cumsum(log_keep, axis=3, reverse=True) - log_keep
    a = jnp.where(mask, jnp.exp(jax.nn.log_sigmoid(z) + after), 0.0)
    return jnp.einsum('bhqk,bkhd->bqhd', a.astype(v.dtype), v)


def forgetting_attention(q, k, v, fq, fk, qpos, kpos):
    s = jnp.einsum('bqhd,bkhd->bhqk', q, k).astype(jnp.float32) * (q.shape[-1] ** -0.5)
    s = s + jnp.swapaxes(fq, 1, 2)[..., :, None] - jnp.swapaxes(fk, 1, 2)[..., None, :]
    p = masked_softmax(s, kpos[None, :] <= qpos[:, None])
    return jnp.einsum('bhqk,bkhd->bqhd', p.astype(v.dtype), v)


def even_project(h, pos, w_in):
    B, T, _ = h.shape
    proj = jnp.einsum('btd,de->bte', h, w_in)
    cuts = np.cumsum(EVEN_SPLIT_SIZES)[:-1].tolist()
    q_n, kv_n, gate_n, z_n, qkv_s, z_s = jnp.split(proj, cuts, axis=-1)
    q_n = rope(q_n.reshape(B, T, NSA_H, HEAD_DIM), pos).reshape(B, T, NSA_KVH, NSA_GROUP, HEAD_DIM)
    kv_n = kv_n.reshape(B, T, 3, 2, NSA_KVH, HEAD_DIM)
    kv_n = jnp.stack([rope(kv_n[:, :, :, 0], pos), kv_n[:, :, :, 1]], axis=3)
    gates = jax.nn.sigmoid(gate_n.reshape(B, T, NSA_KVH, NSA_GROUP, 3))
    qkv_s = qkv_s.reshape(B, T, 3, SB_H, HEAD_DIM)
    return q_n, kv_n, gates, z_n, qkv_s, z_s


def even_output(o_n, z_n, o_s, z_s, w_out):
    B, T = o_n.shape[:2]
    mix = jnp.concatenate([o_n.reshape(B, T, -1) * jax.nn.silu(z_n),
                           o_s.reshape(B, T, -1) * jax.nn.silu(z_s)], axis=-1)
    return jnp.einsum('bte,ed->btd', mix, w_out)


def compress_kv(cmp_rows, cmp_pe, cmp_w1, cmp_w2):
    kc = nsa_compress(cmp_rows[:, :, 0], cmp_pe[0], cmp_w1[0], cmp_w2[0])
    vc = nsa_compress(cmp_rows[:, :, 1], cmp_pe[1], cmp_w1[1], cmp_w2[1])
    return kc, vc


def even_layer_prompt(h, pos, w_in, cmp_pe, cmp_w1, cmp_w2, w_out):
    B, T, _ = h.shape
    q_n, kv_n, gates, z_n, qkv_s, z_s = even_project(h, pos, w_in)
    kc, vc = compress_kv(kv_n[:, :, 0], cmp_pe, cmp_w1, cmp_w2)
    sk = to_sel_blocks(kv_n[:, :, 1, 0])
    sv = to_sel_blocks(kv_n[:, :, 1, 1])
    kw_pad = jnp.pad(kv_n[:, :, 2], ((0, 0), (WINDOW, 0), (0, 0), (0, 0), (0, 0)))
    nb = T // Q_BLOCK

    def nsa_block(i):
        start = i * Q_BLOCK
        qb = lax.dynamic_slice_in_dim(q_n, start, Q_BLOCK, axis=1)
        gb = lax.dynamic_slice_in_dim(gates, start, Q_BLOCK, axis=1)
        kwb = lax.dynamic_slice_in_dim(kw_pad, start, WINDOW + Q_BLOCK, axis=1)
        qpos = start + jnp.arange(Q_BLOCK)
        wpos = start - WINDOW + jnp.arange(WINDOW + Q_BLOCK)
        return nsa_attend(qb, gb, qpos, kc, vc, sk, sv, kwb[:, :, 0], kwb[:, :, 1], wpos)

    o_n = block_sweep(nsa_block, nb, B)
    kpos = jnp.arange(T)

    def sb_block(i):
        start = i * Q_BLOCK
        qb = lax.dynamic_slice_in_dim(qkv_s[:, :, 0], start, Q_BLOCK, axis=1)
        return stick_breaking(qb, qkv_s[:, :, 1], qkv_s[:, :, 2], start + jnp.arange(Q_BLOCK), kpos)

    o_s = block_sweep(sb_block, nb, B)
    y = even_output(o_n, z_n, o_s, z_s, w_out)
    wbp = min(WINDOW, T)
    return y, (kv_n[:, :, 0], kv_n[:, :, 1], kv_n[:, T - wbp:, 2], qkv_s[:, :, 1:3])


def even_layer_sample(h, pos, past_len, cmp_pool, sel_pool, win_buf, sb_pool, page_table,
                      w_in, cmp_pe, cmp_w1, cmp_w2, w_out):
    B, T, _ = h.shape
    q_n, kv_n, gates, z_n, qkv_s, z_s = even_project(h, pos, w_in)
    L = past_len + T
    l_pad = ((L + SEL_BLOCK - 1) // SEL_BLOCK) * SEL_BLOCK
    padw = ((0, 0), (0, l_pad - L), (0, 0), (0, 0), (0, 0))
    cmp_rows = jnp.pad(jnp.concatenate([gather_pages(cmp_pool, page_table), kv_n[:, :, 0]], axis=1), padw)
    sel_rows = jnp.pad(jnp.concatenate([gather_pages(sel_pool, page_table), kv_n[:, :, 1]], axis=1), padw)
    kc, vc = compress_kv(cmp_rows, cmp_pe, cmp_w1, cmp_w2)
    sk = to_sel_blocks(sel_rows[:, :, 0])
    sv = to_sel_blocks(sel_rows[:, :, 1])
    win_rows = jnp.concatenate([win_buf, kv_n[:, :, 2]], axis=1)
    wb = win_buf.shape[1]
    wpos = past_len - wb + jnp.arange(wb + T)
    o_n = nsa_attend(q_n, gates, pos, kc, vc, sk, sv, win_rows[:, :, 0], win_rows[:, :, 1], wpos)
    sb_rows = jnp.concatenate([gather_pages(sb_pool, page_table), qkv_s[:, :, 1:3]], axis=1)
    o_s = stick_breaking(qkv_s[:, :, 0], sb_rows[:, :, 0], sb_rows[:, :, 1], pos, jnp.arange(L))
    y = even_output(o_n, z_n, o_s, z_s, w_out)
    return y, (kv_n[:, :, 0], kv_n[:, :, 1], win_rows[:, T:], qkv_s[:, :, 1:3])


def odd_project(h, w_in, b_f):
    B, T, _ = h.shape
    proj = jnp.einsum('btd,de->bte', h, w_in)
    c1 = 3 * FOX_H * HEAD_DIM
    qkv, f_logit, z = jnp.split(proj, [c1, c1 + FOX_H], axis=-1)
    qkv = qkv.reshape(B, T, 3, FOX_H, HEAD_DIM)
    logf = jax.nn.log_sigmoid((f_logit + b_f).astype(jnp.float32))
    return qkv, logf, z


def odd_layer_prompt(h, w_in, b_f, w_out):
    B, T, _ = h.shape
    qkv, logf, z = odd_project(h, w_in, b_f)
    F = lax.cumsum(logf, axis=1)
    kpos = jnp.arange(T)

    def fox_block(i):
        start = i * Q_BLOCK
        qb = lax.dynamic_slice_in_dim(qkv[:, :, 0], start, Q_BLOCK, axis=1)
        fq = lax.dynamic_slice_in_dim(F, start, Q_BLOCK, axis=1)
        return forgetting_attention(qb, qkv[:, :, 1], qkv[:, :, 2], fq, F, start + jnp.arange(Q_BLOCK), kpos)

    o = block_sweep(fox_block, T // Q_BLOCK, B).reshape(B, T, ODD_MIX)
    y = jnp.einsum('bte,ed->btd', o * jax.nn.silu(z), w_out)
    return y, (qkv[:, :, 1:3], logf)


def odd_layer_sample(h, pos, past_len, kv_pool, logf_pool, page_table, w_in, b_f, w_out):
    B, T, _ = h.shape
    qkv, logf, z = odd_project(h, w_in, b_f)
    kv_rows = jnp.concatenate([gather_pages(kv_pool, page_table), qkv[:, :, 1:3]], axis=1)
    logf_rows = jnp.concatenate([gather_pages(logf_pool, page_table).astype(jnp.float32), logf], axis=1)
    F = lax.cumsum(logf_rows, axis=1)
    o = forgetting_attention(qkv[:, :, 0], kv_rows[:, :, 0], kv_rows[:, :, 1], F[:, past_len:], F,
                             pos, jnp.arange(past_len + T)).reshape(B, T, ODD_MIX)
    y = jnp.einsum('bte,ed->btd', o * jax.nn.silu(z), w_out)
    return y, (qkv[:, :, 1:3], logf)


def mem_kv(mem, g_mem, w_kv):
    B, M, _ = mem.shape
    kv = jnp.einsum('bmd,de->bme', rmsnorm(mem, g_mem), w_kv)
    return kv.reshape(B, M, 2, MEM_H, MEM_HD)


def mem_attend(h, kv, w_qz, w_o):
    B, T, _ = h.shape
    q, z = jnp.split(jnp.einsum('btd,de->bte', h, w_qz), 2, axis=-1)
    q = q.reshape(B, T, MEM_H, MEM_HD)
    s = jnp.einsum('bthd,bmhd->bhtm', q, kv[:, :, 0]).astype(jnp.float32) * (MEM_HD ** -0.5)
    p = jax.nn.softmax(s, axis=-1)
    o = jnp.einsum('bhtm,bmhd->bthd', p.astype(kv.dtype), kv[:, :, 1]).reshape(B, T, MEM_H * MEM_HD)
    return jnp.einsum('bte,ed->btd', o * jax.nn.silu(z), w_o)


def setup_inputs(seed: int = 0) -> dict:
    key = jax.random.key(seed)
    ks = iter(jax.random.split(key, 32))

    def nrm(shape, scale):
        return scale * jax.random.normal(next(ks), shape, jnp.float32)

    n_pages = PAST_LEN // PAGE_SIZE
    n_used = DEC_BATCH * n_pages
    n_pool = n_used + max(1, n_used // 4)
    wb = min(WINDOW, PAST_LEN)
    page_table = jax.random.permutation(next(ks), n_pool)[:n_used].reshape(DEC_BATCH, n_pages).astype(jnp.int32)
    return {
        "x_prompt": nrm((BATCH, SEQ, D_MODEL), 1.0),
        "x_sample": nrm((DEC_BATCH, DEC_SEQ, D_MODEL), 1.0),
        "cache_nsa_cmp": nrm((N_EVEN, n_pool, PAGE_SIZE, 2, NSA_KVH, HEAD_DIM), 1.0),
        "cache_nsa_sel": nrm((N_EVEN, n_pool, PAGE_SIZE, 2, NSA_KVH, HEAD_DIM), 1.0),
        "cache_nsa_win": nrm((N_EVEN, DEC_BATCH, wb, 2, NSA_KVH, HEAD_DIM), 1.0),
        "cache_sb": nrm((N_EVEN, n_pool, PAGE_SIZE, 2, SB_H, HEAD_DIM), 1.0),
        "cache_fox_kv": nrm((N_ODD, n_pool, PAGE_SIZE, 2, FOX_H, HEAD_DIM), 1.0),
        "cache_fox_logf": jax.nn.log_sigmoid(2.0 + nrm((N_ODD, n_pool, PAGE_SIZE, FOX_H), 1.0)),
        "cache_mem": nrm((DEPTH, DEC_BATCH, MEM_LEN, 2, MEM_H, MEM_HD), 1.0),
        "page_table": page_table,
        "mem_prompt": nrm((BATCH, MEM_LEN, D_MODEL), 1.0),
        "ev_norm": 1.0 + nrm((N_EVEN, D_MODEL), 0.05),
        "ev_w_in": nrm((N_EVEN, D_MODEL, EVEN_COLS), D_MODEL ** -0.5),
        "ev_cmp_pe": nrm((N_EVEN, 2, CMP_BLOCK, HEAD_DIM), 0.1),
        "ev_cmp_w1": nrm((N_EVEN, 2, NSA_KVH, CMP_BLOCK * HEAD_DIM, HEAD_DIM), (CMP_BLOCK * HEAD_DIM) ** -0.5),
        "ev_cmp_w2": nrm((N_EVEN, 2, NSA_KVH, HEAD_DIM, HEAD_DIM), HEAD_DIM ** -0.5),
        "ev_w_out": nrm((N_EVEN, EVEN_MIX, D_MODEL), EVEN_MIX ** -0.5),
        "od_norm": 1.0 + nrm((N_ODD, D_MODEL), 0.05),
        "od_w_in": nrm((N_ODD, D_MODEL, ODD_COLS), D_MODEL ** -0.5),
        "od_b_f": 2.0 + nrm((N_ODD, FOX_H), 0.1),
        "od_w_out": nrm((N_ODD, ODD_MIX, D_MODEL), ODD_MIX ** -0.5),
        "mx_norm": 1.0 + nrm((DEPTH, D_MODEL), 0.05),
        "mx_mem_norm": 1.0 + nrm((DEPTH, D_MODEL), 0.05),
        "mx_w_qz": nrm((DEPTH, D_MODEL, 2 * MEM_H * MEM_HD), D_MODEL ** -0.5),
        "mx_w_kv": nrm((DEPTH, D_MODEL, 2 * MEM_H * MEM_HD), D_MODEL ** -0.5),
        "mx_w_o": nrm((DEPTH, MEM_H * MEM_HD, D_MODEL), (MEM_H * MEM_HD) ** -0.5),
        "final_norm": 1.0 + nrm((D_MODEL,), 0.05),
    }


def reference(x_prompt, x_sample, cache_nsa_cmp, cache_nsa_sel, cache_nsa_win, cache_sb, cache_fox_kv,
              cache_fox_logf, cache_mem, page_table, mem_prompt, ev_norm, ev_w_in, ev_cmp_pe, ev_cmp_w1,
              ev_cmp_w2, ev_w_out, od_norm, od_w_in, od_b_f, od_w_out, mx_norm, mx_mem_norm, mx_w_qz,
              mx_w_kv, mx_w_o, final_norm):
    t_p = x_prompt.shape[1]
    t_s = x_sample.shape[1]
    past_len = page_table.shape[1] * PAGE_SIZE
    pos_p = jnp.arange(t_p, dtype=jnp.int32)
    pos_s = past_len + jnp.arange(t_s, dtype=jnp.int32)
    xp, xs = x_prompt, x_sample
    cmp_p, cmp_s, sel_p, sel_s, win_p, win_s, sb_p, sb_s = [], [], [], [], [], [], [], []
    fkv_p, fkv_s, flf_p, flf_s, mem_p = [], [], [], [], []
    for i in range(DEPTH):
        if i % 2 == 0:
            e = i // 2
            yp, (c, s_, w_, b_) = even_layer_prompt(rmsnorm(xp, ev_norm[e]), pos_p, ev_w_in[e], ev_cmp_pe[e],
                                                    ev_cmp_w1[e], ev_cmp_w2[e], ev_w_out[e])
            cmp_p.append(c); sel_p.append(s_); win_p.append(w_); sb_p.append(b_)
            ys, (c, s_, w_, b_) = even_layer_sample(rmsnorm(xs, ev_norm[e]), pos_s, past_len, cache_nsa_cmp[e],
                                                    cache_nsa_sel[e], cache_nsa_win[e], cache_sb[e], page_table,
                                                    ev_w_in[e], ev_cmp_pe[e], ev_cmp_w1[e], ev_cmp_w2[e], ev_w_out[e])
            cmp_s.append(c); sel_s.append(s_); win_s.append(w_); sb_s.append(b_)
        else:
            o = i // 2
            yp, (kv, lf) = odd_layer_prompt(rmsnorm(xp, od_norm[o]), od_w_in[o], od_b_f[o], od_w_out[o])
            fkv_p.append(kv); flf_p.append(lf)
            ys, (kv, lf) = odd_layer_sample(rmsnorm(xs, od_norm[o]), pos_s, past_len, cache_fox_kv[o],
                                            cache_fox_logf[o], page_table, od_w_in[o], od_b_f[o], od_w_out[o])
            fkv_s.append(kv); flf_s.append(lf)
        xp = xp + yp
        xs = xs + ys
        kv_mem = mem_kv(mem_prompt, mx_mem_norm[i], mx_w_kv[i])
        mem_p.append(kv_mem)
        xp = xp + mem_attend(rmsnorm(xp, mx_norm[i]), kv_mem, mx_w_qz[i], mx_w_o[i])
        xs = xs + mem_attend(rmsnorm(xs, mx_norm[i]), cache_mem[i], mx_w_qz[i], mx_w_o[i])
    y_prompt = rmsnorm(xp, final_norm)
    y_sample = rmsnorm(xs, final_norm)
    return (y_prompt, y_sample,
            jnp.stack(cmp_p), jnp.stack(cmp_s), jnp.stack(sel_p), jnp.stack(sel_s),
            jnp.stack(win_p), jnp.stack(win_s), jnp.stack(sb_p), jnp.stack(sb_s),
            jnp.stack(fkv_p), jnp.stack(fkv_s), jnp.stack(flf_p), jnp.stack(flf_s),
            jnp.stack(mem_p))
```

```python
import functools
import math

import jax
import jax.numpy as jnp
from jax import lax
from jax.experimental import pallas as pl
from jax.experimental.pallas import tpu as pltpu

F32 = jnp.float32
BF16 = jnp.bfloat16

HEAD_DIM = 128
PAGE_SIZE = 128
NSA_GROUP = 4
CMP_BLOCK = 32
CMP_STRIDE = 16
SEL_BLOCK = 64
SEL_TOPK = 16
WINDOW = 512
MEM_H = 4
ROPE_THETA = 10000.0
EPS = 1e-6
NEG = -1e30
SB_DEAD = -150.0

LANES = 128
VMEM_LIMIT = 52 * 1024 * 1024


def _cparams(n_axes):
    return pltpu.CompilerParams(dimension_semantics=("arbitrary",) * n_axes,
                                vmem_limit_bytes=VMEM_LIMIT)


def _sigmoid(x):
    return 1.0 / (1.0 + jnp.exp(-x))


def _softplus(x):
    return jnp.maximum(x, 0.0) + jnp.log(1.0 + jnp.exp(-jnp.abs(x)))


def _dot(a, b):
    return jnp.dot(a, b, preferred_element_type=F32)


def _dot_nt(a, b):
    return lax.dot_general(a, b, (((1,), (1,)), ((), ())), preferred_element_type=F32)


def _split3(x):
    h1 = x.astype(BF16)
    r1 = x - h1.astype(F32)
    h2 = r1.astype(BF16)
    h3 = (r1 - h2.astype(F32)).astype(BF16)
    return h1, h2, h3


def _dot_exact01(x, m01, parts=3):
    hs = _split3(x)[:parts]
    acc = _dot(hs[0], m01)
    for h in hs[1:]:
        acc = acc + _dot(h, m01)
    return acc


def _rope_tables(pos):
    half = HEAD_DIM // 2
    inv = ROPE_THETA ** (-jnp.arange(half, dtype=F32) / half)
    ang = pos.astype(F32)[:, None] * inv[None, :]
    cos = jnp.cos(ang)
    sin = jnp.sin(ang)
    return jnp.concatenate([cos, cos], axis=1), jnp.concatenate([-sin, sin], axis=1)


def _proj_kernel(*refs, rope_heads, has_bias, act):
    x_ref, g_ref, w_ref = refs[:3]
    k = 3
    if any(rope_heads):
        cos_ref, sin_ref = refs[k], refs[k + 1]
        k += 2
    if has_bias:
        b_ref = refs[k]
        k += 1
    o_ref, xn_ref = refs[k], refs[k + 1]

    @pl.when(pl.program_id(1) == 0)
    def _():
        x = x_ref[...]
        r = lax.rsqrt(jnp.mean(x * x, axis=-1, keepdims=True) + EPS)
        xn_ref[...] = ((x * r) * g_ref[...]).astype(BF16)

    acc = _dot(xn_ref[...], w_ref[...])
    if has_bias:
        acc = acc + b_ref[...]
    if act == "sigmoid":
        acc = _sigmoid(acc)
    elif act == "logsigmoid":
        acc = -_softplus(-acc)
    if any(rope_heads):
        cos2 = cos_ref[...]
        sin2 = sin_ref[...]
        for hh, flag in enumerate(rope_heads):
            sl = slice(hh * HEAD_DIM, (hh + 1) * HEAD_DIM)
            xh = acc[:, sl]
            if flag:
                xh = xh * cos2 + pltpu.roll(xh, HEAD_DIM // 2, axis=1) * sin2
            o_ref[:, sl] = xh
    else:
        o_ref[...] = acc


def _norm_proj(x, g, w, *, tm, tn, rope=None, rope_heads=(), bias=None, act=None):
    M, D = x.shape
    N = w.shape[1]
    tm = min(tm, M)
    tn = min(tn, N)
    assert M % tm == 0 and N % tn == 0
    in_specs = [pl.BlockSpec((tm, D), lambda i, j: (i, 0)),
                pl.BlockSpec((1, D), lambda i, j: (0, 0)),
                pl.BlockSpec((D, tn), lambda i, j: (0, j))]
    args = [x, g.reshape(1, D), w]
    if any(rope_heads):
        assert len(rope_heads) * HEAD_DIM == tn
        in_specs += [pl.BlockSpec((tm, HEAD_DIM), lambda i, j: (i, 0))] * 2
        args += list(rope)
    if bias is not None:
        in_specs.append(pl.BlockSpec((1, tn), lambda i, j: (0, j)))
        args.append(bias.reshape(1, N))
    return pl.pallas_call(
        functools.partial(_proj_kernel, rope_heads=tuple(rope_heads), has_bias=bias is not None, act=act),
        out_shape=jax.ShapeDtypeStruct((M, N), F32),
        grid=(M // tm, N // tn),
        in_specs=in_specs,
        out_specs=pl.BlockSpec((tm, tn), lambda i, j: (i, j)),
        scratch_shapes=[pltpu.VMEM((tm, D), BF16)],
        compiler_params=_cparams(2),
        name="norm_proj",
    )(*args)


def _out_kernel(*refs, widths):
    n = len(widths)
    o_refs = refs[:n]
    z_ref, w_ref, res_ref, y_ref, mix_ref = refs[n:n + 5]

    @pl.when(pl.program_id(1) == 0)
    def _():
        off = 0
        for o_ref, wd in zip(o_refs, widths):
            z = z_ref[:, off:off + wd]
            mix_ref[:, off:off + wd] = (o_ref[...] * (z * _sigmoid(z))).astype(BF16)
            off += wd

    y_ref[...] = res_ref[...] + _dot(mix_ref[...], w_ref[...])


def _gated_out(o_parts, z, z_block, w, res, *, tm, tn):
    M, N = res.shape
    widths = tuple(o.shape[1] for o in o_parts)
    E = sum(widths)
    assert w.shape == (E, N)
    tm = min(tm, M)
    tn = min(tn, N)
    assert M % tm == 0 and N % tn == 0
    in_specs = [pl.BlockSpec((tm, wd), lambda i, j: (i, 0)) for wd in widths]
    in_specs += [pl.BlockSpec((tm, E), lambda i, j: (i, z_block)),
                 pl.BlockSpec((E, tn), lambda i, j: (0, j)),
                 pl.BlockSpec((tm, tn), lambda i, j: (i, j))]
    return pl.pallas_call(
        functools.partial(_out_kernel, widths=widths),
        out_shape=jax.ShapeDtypeStruct((M, N), F32),
        grid=(M // tm, N // tn),
        in_specs=in_specs,
        out_specs=pl.BlockSpec((tm, tn), lambda i, j: (i, j)),
        scratch_shapes=[pltpu.VMEM((tm, E), BF16)],
        compiler_params=_cparams(2),
        name="gated_out",
    )(*o_parts, z, w, res)


def _norm_kernel(x_ref, g_ref, o_ref):
    x = x_ref[...]
    r = lax.rsqrt(jnp.mean(x * x, axis=-1, keepdims=True) + EPS)
    o_ref[...] = (x * r) * g_ref[...]


def _rmsnorm(x, g, *, tm):
    M, D = x.shape
    tm = min(tm, M)
    return pl.pallas_call(
        _norm_kernel,
        out_shape=jax.ShapeDtypeStruct((M, D), F32),
        grid=(M // tm,),
        in_specs=[pl.BlockSpec((tm, D), lambda i: (i, 0)), pl.BlockSpec((1, D), lambda i: (0, 0))],
        out_specs=pl.BlockSpec((tm, D), lambda i: (i, 0)),
        compiler_params=_cparams(1),
        name="final_norm",
    )(x, g.reshape(1, D))


def _cumsum_kernel(x_ref, o_ref, carry_ref, *, tt):
    @pl.when(pl.program_id(0) == 0)
    def _():
        carry_ref[...] = jnp.zeros_like(carry_ref)

    r = lax.broadcasted_iota(jnp.int32, (tt, tt), 0)
    c = lax.broadcasted_iota(jnp.int32, (tt, tt), 1)
    tri = jnp.where(c <= r, 1.0, 0.0).astype(BF16)
    x = x_ref[...]
    h1, h2, h3 = _split3(x)
    cs = _dot(tri, h1) + _dot(tri, h2) + _dot(tri, h3) + carry_ref[...]
    o_ref[...] = cs
    carry_ref[...] = cs[tt - 1:tt, :]


def _cumsum_rows(x, *, tt=256):
    T, N = x.shape
    tt = min(tt, T)
    assert T % tt == 0
    return pl.pallas_call(
        functools.partial(_cumsum_kernel, tt=tt),
        out_shape=jax.ShapeDtypeStruct((T, N), F32),
        grid=(T // tt,),
        in_specs=[pl.BlockSpec((tt, N), lambda i: (i, 0))],
        out_specs=pl.BlockSpec((tt, N), lambda i: (i, 0)),
        scratch_shapes=[pltpu.VMEM((1, N), F32)],
        compiler_params=_cparams(1),
        name="cumsum_rows",
    )(x)


def _fox_kernel(q_ref, k_ref, v_ref, fq_ref, fk_ref, o_ref, *, tq, tk):
    i = pl.program_id(1)
    scale = HEAD_DIM ** -0.5
    q = q_ref[...].astype(BF16)
    fq = jnp.concatenate([fq_ref[...]] * (tk // LANES), axis=1)

    def tile(kt, carry, masked):
        m, l, acc = carry
        start = pl.multiple_of(kt * tk, tk)
        k = k_ref[pl.ds(start, tk), :].astype(BF16)
        v = v_ref[pl.ds(start, tk), :].astype(BF16)
        s = _dot_nt(q, k) * scale
        s = s + fq - fk_ref[:, pl.ds(start, tk)]
        if masked:
            qpos = i * tq + lax.broadcasted_iota(jnp.int32, (tq, tk), 0)
            kpos = start + lax.broadcasted_iota(jnp.int32, (tq, tk), 1)
            valid = kpos <= qpos
            s = jnp.where(valid, s, NEG)
        m_new = jnp.maximum(m, jnp.max(s, axis=1, keepdims=True))
        e = jnp.exp(s - m_new)
        if masked:
            e = jnp.where(valid, e, 0.0)
        alpha = jnp.exp(m - m_new)
        l = alpha * l + jnp.sum(e, axis=1, keepdims=True)
        acc = alpha * acc + _dot(e.astype(BF16), v)
        return m_new, l, acc

    init = (jnp.full((tq, 1), NEG, F32), jnp.zeros((tq, 1), F32), jnp.zeros((tq, HEAD_DIM), F32))
    n_full = (i * tq) // tk
    carry = lax.fori_loop(0, n_full, lambda kt, c: tile(kt, c, False), init)
    for d in range(tq // tk):
        carry = tile(n_full + d, carry, True)
    m, l, acc = carry
    o_ref[...] = acc / jnp.maximum(l, 1e-30)


def _fox_attention(q, kv, f_cum, n_heads, *, tq=512, tk=512):
    T = q.shape[0]
    tq = min(tq, T)
    tk = min(tk, tq)
    assert T % tq == 0 and tq % tk == 0
    f_t = f_cum.T
    f_row = f_t.reshape(n_heads, 1, T)
    f_rep = jnp.broadcast_to(f_t[:, :, None], (n_heads, T, LANES))
    return pl.pallas_call(
        functools.partial(_fox_kernel, tq=tq, tk=tk),
        out_shape=jax.ShapeDtypeStruct((T, n_heads * HEAD_DIM), F32),
        grid=(n_heads, T // tq),
        in_specs=[pl.BlockSpec((tq, HEAD_DIM), lambda h, i: (i, h)),
                  pl.BlockSpec((T, HEAD_DIM), lambda h, i: (0, h)),
                  pl.BlockSpec((T, HEAD_DIM), lambda h, i: (0, n_heads + h)),
                  pl.BlockSpec((None, tq, LANES), lambda h, i: (h, i, 0)),
                  pl.BlockSpec((None, 1, T), lambda h, i: (h, 0, 0))],
        out_specs=pl.BlockSpec((tq, HEAD_DIM), lambda h, i: (i, h)),
        compiler_params=_cparams(2),
        name="fox_attention",
    )(q, kv, kv, f_rep, f_row)


def _sb_kernel(q_ref, k_ref, v_ref, o_ref, *, tq):
    i = pl.program_id(1)
    tk = tq
    scale = HEAD_DIM ** -0.5
    q = q_ref[...].astype(BF16)
    r = lax.broadcasted_iota(jnp.int32, (tk, 2 * tk), 0)
    c = lax.broadcasted_iota(jnp.int32, (tk, 2 * tk), 1)
    suffix = jnp.where((r > c) | (c >= tk), 1.0, 0.0).astype(BF16)

    def tile(kt, c_in, acc, masked):
        start = pl.multiple_of(kt * tk, tk)
        k = k_ref[pl.ds(start, tk), :].astype(BF16)
        v = v_ref[pl.ds(start, tk), :].astype(BF16)
        z = _dot_nt(q, k) * scale
        sp = _softplus(z)
        log_keep = -sp
        if masked:
            valid = lax.broadcasted_iota(jnp.int32, (tq, tk), 1) < lax.broadcasted_iota(jnp.int32, (tq, tk), 0)
            log_keep = jnp.where(valid, log_keep, 0.0)
        sums = _dot_exact01(log_keep, suffix, parts=2)
        after = sums[:, :tk] + c_in
        a = jnp.exp((z - sp) + after)
        if masked:
            a = jnp.where(valid, a, 0.0)
        acc = acc + _dot(a.astype(BF16), v)
        return c_in + sums[:, tk:], acc

    c0, acc0 = tile(i, jnp.zeros((tq, tk), F32), jnp.zeros((tq, HEAD_DIM), F32), True)

    def cond(st):
        kt, alive, _, _ = st
        return jnp.logical_and(kt >= 0, alive)

    def body(st):
        kt, _, c_in, acc = st
        c_out, acc = tile(kt, c_in, acc, False)
        return kt - 1, jnp.max(c_out) > SB_DEAD, c_out, acc

    _, _, _, acc = lax.while_loop(cond, body, (i - 1, jnp.max(c0) > SB_DEAD, c0, acc0))
    o_ref[...] = acc


def _sb_attention(q, kv, n_heads, *, tq=128):
    T = q.shape[0]
    tq = min(tq, T)
    assert T % tq == 0
    return pl.pallas_call(
        functools.partial(_sb_kernel, tq=tq),
        out_shape=jax.ShapeDtypeStruct((T, n_heads * HEAD_DIM), F32),
        grid=(n_heads, T // tq),
        in_specs=[pl.BlockSpec((tq, HEAD_DIM), lambda h, i: (i, h)),
                  pl.BlockSpec((T, HEAD_DIM), lambda h, i: (0, h)),
                  pl.BlockSpec((T, HEAD_DIM), lambda h, i: (0, n_heads + h))],
        out_specs=pl.BlockSpec((tq, HEAD_DIM), lambda h, i: (i, h)),
        compiler_params=_cparams(2),
        name="sb_attention",
    )(q, kv, kv)


N_CMP_PARTS = CMP_BLOCK // CMP_STRIDE
SEL_PER_CMP = SEL_BLOCK // CMP_STRIDE


def _compress_mlp(chunk_sums, w2_ref, o_ref):
    n = chunk_sums[0].shape[0]
    h = chunk_sums[0]
    for p in range(1, N_CMP_PARTS):
        h = h + pltpu.roll(chunk_sums[p], n - p, axis=0)
    h = h * _sigmoid(h)
    o_ref[...] = _dot(h.astype(BF16), w2_ref[...])


def _compress_kernel(rows_ref, pe_ref, w1_ref, w2_ref, o_ref, *, n_chunks):
    sums = []
    for p in range(N_CMP_PARTS):
        acc = jnp.zeros((n_chunks, HEAD_DIM), F32)
        for j in range(CMP_STRIDE):
            jj = p * CMP_STRIDE + j
            x = rows_ref[pl.ds(j, n_chunks, stride=CMP_STRIDE), :]
            acc = acc + _dot((x + pe_ref[jj:jj + 1, :]).astype(BF16), w1_ref[jj])
        sums.append(acc)
    _compress_mlp(sums, w2_ref, o_ref)


def _nsa_compress(kv_n, pe, w1, w2):
    T = kv_n.shape[0]
    G = w1.shape[1]
    n_chunks = T // CMP_STRIDE
    return pl.pallas_call(
        functools.partial(_compress_kernel, n_chunks=n_chunks),
        out_shape=jax.ShapeDtypeStruct((2, G, n_chunks, HEAD_DIM), F32),
        grid=(2, G),
        in_specs=[pl.BlockSpec((T, HEAD_DIM), lambda a, g: (0, a * G + g)),
                  pl.BlockSpec((None, CMP_BLOCK, HEAD_DIM), lambda a, g: (a, 0, 0)),
                  pl.BlockSpec((None, None, CMP_BLOCK, HEAD_DIM, HEAD_DIM), lambda a, g: (a, g, 0, 0, 0)),
                  pl.BlockSpec((None, None, HEAD_DIM, HEAD_DIM), lambda a, g: (a, g, 0, 0))],
        out_specs=pl.BlockSpec((None, None, n_chunks, HEAD_DIM), lambda a, g: (a, g, 0, 0)),
        compiler_params=_cparams(2),
        name="nsa_compress",
    )(kv_n, pe, w1, w2)


def _stack_heads(q):
    r = q.shape[1] // HEAD_DIM
    return jnp.concatenate([q[:, a * HEAD_DIM:(a + 1) * HEAD_DIM] for a in range(r)], axis=0)


def _unstack_heads(o_ref, o, tq):
    for a in range(o.shape[0] // tq):
        o_ref[:, a * HEAD_DIM:(a + 1) * HEAD_DIM] = o[a * tq:(a + 1) * tq, :]


def _select_blocks(imp_sel, qpos, ns, nsp):
    rows = imp_sel.shape[0]
    blk_i = lax.broadcasted_iota(jnp.int32, (rows, nsp), 1)
    blk = blk_i.astype(F32)
    visible = blk_i * SEL_BLOCK <= qpos
    is_cur = blk_i == qpos // SEL_BLOCK
    score = jnp.where(is_cur, 1e9, jnp.where(visible, imp_sel, -1e9))
    score = jnp.where(blk_i < ns, score, -3e38)

    def pick(_, st):
        score, sel = st
        m = jnp.max(score, axis=1, keepdims=True)
        idx = jnp.min(jnp.where(score == m, blk, float(nsp)), axis=1, keepdims=True)
        hit = blk == idx
        sel = jnp.where(hit & (m > -1e8), 1.0, sel)
        return jnp.where(hit, -3e38, score), sel

    _, sel = lax.fori_loop(0, min(SEL_TOPK, ns), pick, (score, jnp.zeros((rows, nsp), F32)))
    return sel


def _cmp_to_sel_matrix(ncp, nsp, nc):
    c = lax.broadcasted_iota(jnp.int32, (ncp, nsp), 0)
    j = lax.broadcasted_iota(jnp.int32, (ncp, nsp), 1)
    lo = SEL_PER_CMP * j - (N_CMP_PARTS - 1)
    hit = (c >= lo) & (c <= lo + SEL_PER_CMP + N_CMP_PARTS - 2) & (c < nc)
    return jnp.where(hit, 1.0, 0.0).astype(BF16)


def _nsa_cmp_kernel(q_ref, kc_ref, vc_ref, oc_ref, sel_ref, *, tq, nc, ns):
    i = pl.program_id(1)
    ncp = kc_ref.shape[0]
    nsp = sel_ref.shape[1]
    scale = HEAD_DIM ** -0.5
    q = _stack_heads(q_ref[...]).astype(BF16)
    rows = q.shape[0]
    s = _dot_nt(q, kc_ref[...].astype(BF16)) * scale
    qpos = i * tq + (lax.broadcasted_iota(jnp.int32, (rows, ncp), 0) & (tq - 1))
    cidx = lax.broadcasted_iota(jnp.int32, (rows, ncp), 1)
    valid = (cidx * CMP_STRIDE + (CMP_BLOCK - 1) <= qpos) & (cidx < nc)
    s = jnp.where(valid, s, NEG)
    e = jnp.where(valid, jnp.exp(s - jnp.max(s, axis=1, keepdims=True)), 0.0)
    p = e / jnp.maximum(jnp.sum(e, axis=1, keepdims=True), 1e-30)
    _unstack_heads(oc_ref, _dot(p.astype(BF16), vc_ref[...].astype(BF16)), tq)
    imp = p[0:tq]
    for a in range(1, rows // tq):
        imp = imp + p[a * tq:(a + 1) * tq]
    imp_sel = _dot_exact01(imp, _cmp_to_sel_matrix(ncp, nsp, nc))
    qpos1 = i * tq + lax.broadcasted_iota(jnp.int32, (tq, nsp), 0)
    sel_ref[...] = _select_blocks(imp_sel, qpos1, ns, nsp).astype(BF16)


def _nsa_sel_kernel(q_ref, k_ref, v_ref, sel_ref, o_ref, *, tq):
    i = pl.program_id(1)
    tk = tq
    nsp = sel_ref.shape[1]
    scale = HEAD_DIM ** -0.5
    q = _stack_heads(q_ref[...]).astype(BF16)
    heads = q.shape[0] // tq
    sel = sel_ref[...]
    jidx = lax.broadcasted_iota(jnp.int32, (nsp, tk), 0)
    koff = lax.broadcasted_iota(jnp.int32, (nsp, tk), 1) // SEL_BLOCK
    qpos = i * tq + lax.broadcasted_iota(jnp.int32, (tq, tk), 0)
    kidx = lax.broadcasted_iota(jnp.int32, (tq, tk), 1)

    def tile(kt, carry):
        m, l, acc = carry
        start = pl.multiple_of(kt * tk, tk)
        k = k_ref[pl.ds(start, tk), :].astype(BF16)
        v = v_ref[pl.ds(start, tk), :].astype(BF16)
        expand = jnp.where(jidx == kt * (tk // SEL_BLOCK) + koff, 1.0, 0.0).astype(BF16)
        chosen = _dot(sel, expand)
        valid1 = (chosen > 0.5) & (start + kidx <= qpos)
        valid = jnp.concatenate([valid1] * heads, axis=0)
        s = jnp.where(valid, _dot_nt(q, k) * scale, NEG)
        m_new = jnp.maximum(m, jnp.max(s, axis=1, keepdims=True))
        e = jnp.where(valid, jnp.exp(s - m_new), 0.0)
        alpha = jnp.exp(m - m_new)
        l = alpha * l + jnp.sum(e, axis=1, keepdims=True)
        acc = alpha * acc + _dot(e.astype(BF16), v)
        return m_new, l, acc

    rows = q.shape[0]
    init = (jnp.full((rows, 1), NEG, F32), jnp.zeros((rows, 1), F32), jnp.zeros((rows, HEAD_DIM), F32))
    _, l, acc = lax.fori_loop(0, i + 1, tile, init)
    _unstack_heads(o_ref, acc / jnp.maximum(l, 1e-30), tq)


def _nsa_win_kernel(q_ref, k_ref, v_ref, oc_ref, os_ref, gate_ref, o_ref, *, tq):
    i = pl.program_id(1)
    tk = tq
    scale = HEAD_DIM ** -0.5
    q = _stack_heads(q_ref[...]).astype(BF16)
    heads = q.shape[0] // tq
    qpos = i * tq + lax.broadcasted_iota(jnp.int32, (tq, tk), 0)
    kidx = lax.broadcasted_iota(jnp.int32, (tq, tk), 1)

    def tile(kt, carry):
        m, l, acc = carry
        start = pl.multiple_of(kt * tk, tk)
        k = k_ref[pl.ds(start, tk), :].astype(BF16)
        v = v_ref[pl.ds(start, tk), :].astype(BF16)
        kpos = start + kidx
        valid1 = (kpos <= qpos) & (kpos > qpos - WINDOW)
        valid = jnp.concatenate([valid1] * heads, axis=0)
        s = jnp.where(valid, _dot_nt(q, k) * scale, NEG)
        m_new = jnp.maximum(m, jnp.max(s, axis=1, keepdims=True))
        e = jnp.where(valid, jnp.exp(s - m_new), 0.0)
        alpha = jnp.exp(m - m_new)
        l = alpha * l + jnp.sum(e, axis=1, keepdims=True)
        acc = alpha * acc + _dot(e.astype(BF16), v)
        return m_new, l, acc

    rows = q.shape[0]
    init = (jnp.full((rows, 1), NEG, F32), jnp.zeros((rows, 1), F32), jnp.zeros((rows, HEAD_DIM), F32))
    first = jnp.maximum(i - (WINDOW + tk - 1) // tk, 0)
    _, l, acc = lax.fori_loop(first, i + 1, tile, init)
    o_w = acc / jnp.maximum(l, 1e-30)
    gates = gate_ref[...]
    for a in range(heads):
        sl = slice(a * HEAD_DIM, (a + 1) * HEAD_DIM)
        o_ref[:, sl] = (gates[:, 3 * a:3 * a + 1] * oc_ref[:, sl]
                        + gates[:, 3 * a + 1:3 * a + 2] * os_ref[:, sl]
                        + gates[:, 3 * a + 2:3 * a + 3] * o_w[a * tq:(a + 1) * tq, :])


def _nsa_prompt(q_n, kv_n, gates, kc_vc, *, tq=128):
    T = q_n.shape[0]
    G = kc_vc.shape[1]
    gw = q_n.shape[1] // G
    n_chunks = kc_vc.shape[2]
    nc = n_chunks - (N_CMP_PARTS - 1)
    ns = T // SEL_BLOCK
    nsp = -(-ns // LANES) * LANES
    tq = min(tq, T)
    nq = T // tq
    q_spec = pl.BlockSpec((tq, gw), lambda g, i: (i, g))
    o_shape = jax.ShapeDtypeStruct((T, G * gw), F32)

    def kv_spec(branch, a):
        return pl.BlockSpec((T, HEAD_DIM), lambda g, i: (0, (branch * 2 + a) * G + g))

    o_c, sel = pl.pallas_call(
        functools.partial(_nsa_cmp_kernel, tq=tq, nc=nc, ns=ns),
        out_shape=(o_shape, jax.ShapeDtypeStruct((G, T, nsp), BF16)),
        grid=(G, nq),
        in_specs=[q_spec,
                  pl.BlockSpec((None, None, n_chunks, HEAD_DIM), lambda g, i: (0, g, 0, 0)),
                  pl.BlockSpec((None, None, n_chunks, HEAD_DIM), lambda g, i: (1, g, 0, 0))],
        out_specs=(q_spec, pl.BlockSpec((None, tq, nsp), lambda g, i: (g, i, 0))),
        compiler_params=_cparams(2),
        name="nsa_cmp",
    )(q_n, kc_vc, kc_vc)
    o_s = pl.pallas_call(
        functools.partial(_nsa_sel_kernel, tq=tq),
        out_shape=o_shape,
        grid=(G, nq),
        in_specs=[q_spec, kv_spec(1, 0), kv_spec(1, 1), pl.BlockSpec((None, tq, nsp), lambda g, i: (g, i, 0))],
        out_specs=q_spec,
        compiler_params=_cparams(2),
        name="nsa_sel",
    )(q_n, kv_n, kv_n, sel)
    return pl.pallas_call(
        functools.partial(_nsa_win_kernel, tq=tq),
        out_shape=o_shape,
        grid=(G, nq),
        in_specs=[q_spec, kv_spec(2, 0), kv_spec(2, 1), q_spec, q_spec,
                  pl.BlockSpec((tq, LANES), lambda g, i: (i, g))],
        out_specs=q_spec,
        compiler_params=_cparams(2),
        name="nsa_win",
    )(q_n, kv_n, kv_n, o_c, o_s, gates)


def _mem_kernel(q_ref, kv_ref, o_ref):
    scale = HEAD_DIM ** -0.5
    for h in range(MEM_H):
        sl = slice(h * HEAD_DIM, (h + 1) * HEAD_DIM)
        k = kv_ref[:, sl].astype(BF16)
        v = kv_ref[:, MEM_H * HEAD_DIM + h * HEAD_DIM:MEM_H * HEAD_DIM + (h + 1) * HEAD_DIM].astype(BF16)
        s = _dot_nt(q_ref[:, sl].astype(BF16), k) * scale
        e = jnp.exp(s - jnp.max(s, axis=1, keepdims=True))
        p = e / jnp.sum(e, axis=1, keepdims=True)
        o_ref[:, sl] = _dot(p.astype(BF16), v)


def _mem_attention(qz, kv, *, tq=512):
    T = qz.shape[0]
    M = kv.shape[0]
    E = MEM_H * HEAD_DIM
    tq = min(tq, T)
    return pl.pallas_call(
        _mem_kernel,
        out_shape=jax.ShapeDtypeStruct((T, E), F32),
        grid=(T // tq,),
        in_specs=[pl.BlockSpec((tq, E), lambda i: (i, 0)), pl.BlockSpec((M, 2 * E), lambda i: (0, 0))],
        out_specs=pl.BlockSpec((tq, E), lambda i: (i, 0)),
        compiler_params=_cparams(1),
        name="mem_attention",
    )(qz, kv)


ROWS_PAD = 16


def _head_of_lane(rows, hk):
    lane_head = lax.broadcasted_iota(jnp.int32, (rows, hk * HEAD_DIM), 1) // HEAD_DIM
    row = lax.broadcasted_iota(jnp.int32, (rows, hk * HEAD_DIM), 0)
    return lane_head, row


def _block_diag(x, hk, group):
    lane_head, row = _head_of_lane(x.shape[0], hk)
    return jnp.where(lane_head == row // group, jnp.concatenate([x] * hk, axis=1), 0.0)


def _diag_part(acc, hk, group):
    rows = acc.shape[0]
    row = lax.broadcasted_iota(jnp.int32, (rows, HEAD_DIM), 0)
    out = jnp.zeros((rows, HEAD_DIM), F32)
    for h in range(hk):
        out = jnp.where(row // group == h, acc[:, h * HEAD_DIM:(h + 1) * HEAD_DIM], out)
    return out


def _expand_rows(x, group, rows):
    parts = [jnp.broadcast_to(x[h:h + 1, :], (group, HEAD_DIM)) for h in range(x.shape[0])]
    pad = rows - group * x.shape[0]
    if pad:
        parts.append(jnp.zeros((pad, HEAD_DIM), F32))
    return jnp.concatenate(parts, axis=0) if len(parts) > 1 else parts[0]


def _gather_cat(ref, first, hk, n_keys, stride):
    return jnp.concatenate([ref[pl.ds(first + h, n_keys, stride=stride), :] for h in range(hk)],
                           axis=1).astype(BF16)


def _rep(x, hk):
    return jnp.concatenate([x] * hk, axis=1)


def _suffix_matrix(n):
    r = lax.broadcasted_iota(jnp.int32, (n, 2 * n), 0)
    c = lax.broadcasted_iota(jnp.int32, (n, 2 * n), 1)
    return jnp.where((r > c) | (c >= n), 1.0, 0.0).astype(BF16)


def _lane_rep(col, width=LANES):
    return jnp.broadcast_to(col, (col.shape[0], width))


def _fox_decode_kernel(pt_ref, q_ref, kvn_ref, lfn_ref, kv_ref, lf_ref, o_ref, m_ref, l_ref, c_ref, acc_ref, *, hk):
    p = pl.program_id(1)
    scale = HEAD_DIM ** -0.5
    q = q_ref[...]

    @pl.when(p == 0)
    def _():
        s_new = jnp.sum(q * kvn_ref[0:hk, :], axis=1, keepdims=True) * scale
        m_ref[...] = _lane_rep(s_new)
        l_ref[...] = jnp.ones_like(l_ref)
        c_ref[...] = lfn_ref[...]
        acc_ref[...] = _block_diag(kvn_ref[hk:2 * hk, :], hk, 1)

    kcat = _gather_cat(kv_ref, 0, hk, PAGE_SIZE, 2 * hk)
    vcat = _gather_cat(kv_ref, hk, hk, PAGE_SIZE, 2 * hk)
    s = _dot_nt(_block_diag(q, hk, 1).astype(BF16), kcat) * scale
    sums = _dot_exact01(lf_ref[...], _suffix_matrix(PAGE_SIZE))
    s = s + (c_ref[...] + sums[:, :PAGE_SIZE])
    m_old = m_ref[...]
    m_new = jnp.maximum(m_old, jnp.max(s, axis=1, keepdims=True))
    e = jnp.exp(s - m_new)
    alpha = jnp.exp(m_old - m_new)
    l_ref[...] = alpha * l_ref[...] + jnp.sum(e, axis=1, keepdims=True)
    acc_ref[...] = _rep(alpha, hk) * acc_ref[...] + _dot(e.astype(BF16), vcat)
    m_ref[...] = m_new
    c_ref[...] = c_ref[...] + sums[:, PAGE_SIZE:]

    @pl.when(p == pl.num_programs(1) - 1)
    def _():
        o_ref[...] = _diag_part(acc_ref[...], hk, 1) / jnp.maximum(l_ref[...], 1e-30)


def _sb_decode_kernel(pt_ref, q_ref, kv_ref, o_ref, c_ref, acc_ref, *, hk):
    p = pl.program_id(1)
    scale = HEAD_DIM ** -0.5

    @pl.when(p == 0)
    def _():
        row = lax.broadcasted_iota(jnp.int32, c_ref.shape, 0)
        c_ref[...] = jnp.where(row < hk, 0.0, NEG)
        acc_ref[...] = jnp.zeros_like(acc_ref)

    @pl.when(jnp.max(c_ref[...]) > SB_DEAD)
    def _():
        kcat = _gather_cat(kv_ref, 0, hk, PAGE_SIZE, 2 * hk)
        vcat = _gather_cat(kv_ref, hk, hk, PAGE_SIZE, 2 * hk)
        z = _dot_nt(_block_diag(q_ref[...], hk, 1).astype(BF16), kcat) * scale
        sp = _softplus(z)
        sums = _dot_exact01(-sp, _suffix_matrix(PAGE_SIZE), parts=2)
        a = jnp.exp((z - sp) + (c_ref[...] + sums[:, :PAGE_SIZE]))
        acc_ref[...] = acc_ref[...] + _dot(a.astype(BF16), vcat)
        c_ref[...] = c_ref[...] + sums[:, PAGE_SIZE:]

    @pl.when(p == pl.num_programs(1) - 1)
    def _():
        o_ref[...] = _diag_part(acc_ref[...], hk, 1)


def _paged_decode(kern, pool, page_table, layer, small_inputs, extra_specs, extra_args, scratch, *, slots,
                  reverse, name):
    n_pool = pool.shape[1]
    B, n_pages = page_table.shape
    pool2 = pool.reshape(pool.shape[0] * n_pool, PAGE_SIZE * slots, HEAD_DIM)
    pt = page_table.reshape(B * n_pages)

    def page_of(b, p, pt_ref):
        pp = (n_pages - 1 - p) if reverse else p
        return layer * n_pool + pt_ref[b * n_pages + pp]

    in_specs = [pl.BlockSpec((None,) + x.shape[1:], lambda b, p, pt_ref: (b, 0, 0)) for x in small_inputs]
    in_specs.append(pl.BlockSpec((None, PAGE_SIZE * slots, HEAD_DIM), lambda b, p, pt_ref: (page_of(b, p, pt_ref), 0, 0)))
    in_specs += [f(page_of) for f in extra_specs]
    return pl.pallas_call(
        kern,
        out_shape=jax.ShapeDtypeStruct((B, ROWS_PAD, HEAD_DIM), F32),
        grid_spec=pltpu.PrefetchScalarGridSpec(
            num_scalar_prefetch=1,
            grid=(B, n_pages),
            in_specs=in_specs,
            out_specs=pl.BlockSpec((None, ROWS_PAD, HEAD_DIM), lambda b, p, pt_ref: (b, 0, 0)),
            scratch_shapes=scratch),
        compiler_params=_cparams(2),
        name=name,
    )(pt, *small_inputs, pool2, *extra_args)


def _pad_rows(x, rows=ROWS_PAD):
    return jnp.pad(x, ((0, 0), (0, rows - x.shape[1]), (0, 0)))


def _fox_decode(q, kv_new, logf_new, pool, logf_pool, page_table, layer):
    B = q.shape[0]
    hk = pool.shape[4]
    assert hk == ROWS_PAD
    n_pool = pool.shape[1]
    lf_t = jnp.swapaxes(logf_pool, 2, 3).reshape(logf_pool.shape[0] * n_pool, hk, PAGE_SIZE)
    small = [q.reshape(B, hk, HEAD_DIM), kv_new.reshape(B, 2 * hk, HEAD_DIM),
             jnp.broadcast_to(logf_new[:, :, None], (B, hk, LANES))]
    lf_spec = lambda page_of: pl.BlockSpec((None, hk, PAGE_SIZE), lambda b, p, pt_ref: (page_of(b, p, pt_ref), 0, 0))
    scratch = [pltpu.VMEM((ROWS_PAD, LANES), F32)] * 3 + [pltpu.VMEM((ROWS_PAD, hk * HEAD_DIM), F32)]
    o = _paged_decode(functools.partial(_fox_decode_kernel, hk=hk), pool, page_table, layer, small, [lf_spec],
                      [lf_t], scratch, slots=2 * hk, reverse=True, name="fox_decode")
    return o.reshape(B, hk * HEAD_DIM)


def _sb_decode(q, pool, page_table, layer):
    B = q.shape[0]
    hk = pool.shape[4]
    small = [_pad_rows(q.reshape(B, hk, HEAD_DIM))]
    scratch = [pltpu.VMEM((ROWS_PAD, LANES), F32), pltpu.VMEM((ROWS_PAD, hk * HEAD_DIM), F32)]
    o = _paged_decode(functools.partial(_sb_decode_kernel, hk=hk), pool, page_table, layer, small, [], [],
                      scratch, slots=2 * hk, reverse=True, name="sb_decode")
    return o[:, :hk].reshape(B, hk * HEAD_DIM)


def _cmp_decode_kernel(pt_ref, pool_ref, pe_ref, w1_ref, w2_ref, o_ref, x_ref, *, G, nb, n_pages):
    s = pl.program_id(1)
    p = pl.program_id(2)
    cpp = PAGE_SIZE // CMP_STRIDE
    slots = 2 * G
    base = pl.multiple_of((s * n_pages + p) * cpp, cpp)
    for a in range(2):
        for g in range(G):
            for j in range(CMP_STRIDE):
                x_ref[a * G + g, pl.ds(base, cpp), j * HEAD_DIM:(j + 1) * HEAD_DIM] = (
                    pool_ref[pl.ds(j * slots + a * G + g, cpp, stride=CMP_STRIDE * slots), :])

    @pl.when(jnp.logical_and(s == nb - 1, p == n_pages - 1))
    def _():
        for a in range(2):
            for g in range(G):
                x = x_ref[a * G + g]
                sums = [_dot((x + pe_ref[a, part:part + 1, :]).astype(BF16), w1_ref[a, g, part])
                        for part in range(N_CMP_PARTS)]
                _compress_mlp(sums, w2_ref.at[a, g], o_ref.at[a, g])


def _cmp_decode(pool, page_table, layer, pe, w1, w2, *, nb=4):
    n_pool = pool.shape[1]
    G = pool.shape[4]
    slots = 2 * G
    B, n_pages = page_table.shape
    nb = min(nb, B)
    assert B % nb == 0
    cpp = PAGE_SIZE // CMP_STRIDE
    rows = nb * n_pages * cpp
    pool2 = pool.reshape(pool.shape[0] * n_pool, PAGE_SIZE * slots, HEAD_DIM)
    pt = page_table.reshape(B * n_pages)
    pe2 = pe.reshape(2, N_CMP_PARTS, CMP_STRIDE * HEAD_DIM)
    w1r = w1.reshape(2, G, N_CMP_PARTS, CMP_STRIDE * HEAD_DIM, HEAD_DIM)
    return pl.pallas_call(
        functools.partial(_cmp_decode_kernel, G=G, nb=nb, n_pages=n_pages),
        out_shape=jax.ShapeDtypeStruct((2, G, B * n_pages * cpp, HEAD_DIM), F32),
        grid_spec=pltpu.PrefetchScalarGridSpec(
            num_scalar_prefetch=1,
            grid=(B // nb, nb, n_pages),
            in_specs=[pl.BlockSpec((None, PAGE_SIZE * slots, HEAD_DIM),
                                   lambda bb, s, p, pt_ref: (layer * n_pool + pt_ref[(bb * nb + s) * n_pages + p], 0, 0)),
                      pl.BlockSpec(pe2.shape, lambda bb, s, p, pt_ref: (0, 0, 0)),
                      pl.BlockSpec(w1r.shape, lambda bb, s, p, pt_ref: (0, 0, 0, 0, 0)),
                      pl.BlockSpec(w2.shape, lambda bb, s, p, pt_ref: (0, 0, 0, 0))],
            out_specs=pl.BlockSpec((2, G, rows, HEAD_DIM), lambda bb, s, p, pt_ref: (0, 0, bb, 0)),
            scratch_shapes=[pltpu.VMEM((2 * G, rows, CMP_STRIDE * HEAD_DIM), F32)]),
        compiler_params=_cparams(3),
        name="cmp_decode",
    )(pt, pool2, pe2, w1r, w2)


def _nsa_cmp_decode_kernel(q_ref, kcvc_ref, oc_ref, sel_ref, *, G, qpos, nc, ns):
    scale = HEAD_DIM ** -0.5
    group = NSA_GROUP
    ncp = kcvc_ref.shape[2]
    nsp = sel_ref.shape[1]
    q = q_ref[...]
    kcat = jnp.concatenate([kcvc_ref[0, g] for g in range(G)], axis=1).astype(BF16)
    vcat = jnp.concatenate([kcvc_ref[1, g] for g in range(G)], axis=1).astype(BF16)
    s = _dot_nt(_block_diag(q, G, group).astype(BF16), kcat) * scale
    cidx = lax.broadcasted_iota(jnp.int32, s.shape, 1)
    valid = (cidx * CMP_STRIDE + (CMP_BLOCK - 1) <= qpos) & (cidx < nc)
    s = jnp.where(valid, s, NEG)
    e = jnp.where(valid, jnp.exp(s - jnp.max(s, axis=1, keepdims=True)), 0.0)
    p = e / jnp.maximum(jnp.sum(e, axis=1, keepdims=True), 1e-30)
    oc_ref[...] = _diag_part(_dot(p.astype(BF16), vcat), G, group)
    imp = _expand_rows(jnp.concatenate(
        [jnp.sum(p[g * group:(g + 1) * group], axis=0, keepdims=True) for g in range(G)], axis=0), group, ROWS_PAD)
    imp_sel = _dot_exact01(imp, _cmp_to_sel_matrix(ncp, nsp, nc))
    sel_ref[...] = _select_blocks(imp_sel, jnp.full((ROWS_PAD, nsp), qpos, jnp.int32), ns, nsp).astype(BF16)


def _online_tile(s, valid, vcat, m_old, l_old, acc_old, hk):
    s = jnp.where(valid, s, NEG)
    m_new = jnp.maximum(m_old, jnp.max(s, axis=1, keepdims=True))
    e = jnp.where(valid, jnp.exp(s - m_new[:, :1]), 0.0)
    alpha = jnp.exp(m_old - m_new)
    l_new = alpha * l_old + jnp.sum(e, axis=1, keepdims=True)
    acc_new = _rep(alpha, hk) * acc_old + _dot(e.astype(BF16), vcat)
    return m_new, l_new, acc_new


def _nsa_sel_decode_kernel(pt_ref, q_ref, kvn_ref, sel_ref, pool_ref, o_ref, m_ref, l_ref, acc_ref, *, G):
    p = pl.program_id(1)
    scale = HEAD_DIM ** -0.5
    group = NSA_GROUP
    slots = 2 * G
    q = q_ref[...]

    @pl.when(p == 0)
    def _():
        k_new = _expand_rows(kvn_ref[slots:slots + G, :], group, ROWS_PAD)
        v_new = _expand_rows(kvn_ref[slots + G:2 * slots, :], group, ROWS_PAD)
        m_ref[...] = _lane_rep(jnp.sum(q * k_new, axis=1, keepdims=True) * scale)
        l_ref[...] = jnp.ones_like(l_ref)
        acc_ref[...] = _block_diag(v_new, G, group)

    kcat = _gather_cat(pool_ref, 0, G, PAGE_SIZE, slots)
    vcat = _gather_cat(pool_ref, G, G, PAGE_SIZE, slots)
    s = _dot_nt(_block_diag(q, G, group).astype(BF16), kcat) * scale
    nsp = sel_ref.shape[1]
    jidx = lax.broadcasted_iota(jnp.int32, (nsp, PAGE_SIZE), 0)
    koff = lax.broadcasted_iota(jnp.int32, (nsp, PAGE_SIZE), 1) // SEL_BLOCK
    expand = jnp.where(jidx == p * (PAGE_SIZE // SEL_BLOCK) + koff, 1.0, 0.0).astype(BF16)
    valid = _dot(sel_ref[...], expand) > 0.5
    m, l, acc = _online_tile(s, valid, vcat, m_ref[...], l_ref[...], acc_ref[...], G)
    m_ref[...] = m
    l_ref[...] = l
    acc_ref[...] = acc

    @pl.when(p == pl.num_programs(1) - 1)
    def _():
        o_ref[...] = _diag_part(acc, G, group) / jnp.maximum(l, 1e-30)


def _nsa_win_decode_kernel(q_ref, kvn_ref, win_ref, oc_ref, os_ref, gate_ref, o_ref, *, G, wb, past_len):
    scale = HEAD_DIM ** -0.5
    group = NSA_GROUP
    slots = 2 * G
    q = q_ref[...]
    k_new = _expand_rows(kvn_ref[2 * slots:2 * slots + G, :], group, ROWS_PAD)
    v_new = _expand_rows(kvn_ref[2 * slots + G:3 * slots, :], group, ROWS_PAD)
    m0 = _lane_rep(jnp.sum(q * k_new, axis=1, keepdims=True) * scale)
    kcat = _gather_cat(win_ref, 0, G, wb, slots)
    vcat = _gather_cat(win_ref, G, G, wb, slots)
    s = _dot_nt(_block_diag(q, G, group).astype(BF16), kcat) * scale
    wpos = past_len - wb + lax.broadcasted_iota(jnp.int32, s.shape, 1)
    valid = (wpos <= past_len) & (wpos > past_len - WINDOW) & (wpos >= 0)
    _, l, acc = _online_tile(s, valid, vcat, m0, jnp.ones_like(m0), _block_diag(v_new, G, group), G)
    o_w = _diag_part(acc, G, group) / jnp.maximum(l, 1e-30)
    gates = gate_ref[...]
    o_ref[...] = gates[:, 0:1] * oc_ref[...] + gates[:, 1:2] * os_ref[...] + gates[:, 2:3] * o_w


def _nsa_decode(q_n, kv_new, gates, cmp_pool, sel_pool, win_buf, page_table, layer, pe, w1, w2):
    B, n_pages = page_table.shape
    G = cmp_pool.shape[4]
    heads = G * NSA_GROUP
    slots = 2 * G
    past_len = n_pages * PAGE_SIZE
    assert past_len % CMP_STRIDE == 0
    q3 = _pad_rows(q_n.reshape(B, heads, HEAD_DIM))
    kvn3 = _pad_rows(kv_new.reshape(B, 3 * slots, HEAD_DIM))
    g3 = _pad_rows(jnp.pad(gates.reshape(B, heads, 3), ((0, 0), (0, 0), (0, LANES - 3))))
    cpp = PAGE_SIZE // CMP_STRIDE
    ncp = n_pages * cpp
    l_pad = -(-(past_len + 1) // SEL_BLOCK) * SEL_BLOCK
    ns = l_pad // SEL_BLOCK
    nsp = -(-ns // LANES) * LANES
    row_spec = pl.BlockSpec((None, ROWS_PAD, HEAD_DIM), lambda b: (b, 0, 0))
    row_shape = jax.ShapeDtypeStruct((B, ROWS_PAD, HEAD_DIM), F32)

    kcvc = _cmp_decode(cmp_pool, page_table, layer, pe, w1, w2)
    o_c, sel = pl.pallas_call(
        functools.partial(_nsa_cmp_decode_kernel, G=G, qpos=past_len, nc=ncp, ns=ns),
        out_shape=(row_shape, jax.ShapeDtypeStruct((B, ROWS_PAD, nsp), BF16)),
        grid=(B,),
        in_specs=[row_spec, pl.BlockSpec((2, G, ncp, HEAD_DIM), lambda b: (0, 0, b, 0))],
        out_specs=(row_spec, pl.BlockSpec((None, ROWS_PAD, nsp), lambda b: (b, 0, 0))),
        compiler_params=_cparams(1),
        name="nsa_cmp_decode",
    )(q3, kcvc)
    sel_spec = lambda page_of: pl.BlockSpec((None, ROWS_PAD, nsp), lambda b, p, pt_ref: (b, 0, 0))
    scratch = [pltpu.VMEM((ROWS_PAD, LANES), F32)] * 2 + [pltpu.VMEM((ROWS_PAD, G * HEAD_DIM), F32)]
    o_s = _paged_decode(functools.partial(_nsa_sel_decode_kernel, G=G), sel_pool, page_table, layer,
                        [q3, kvn3, sel], [], [], scratch, slots=slots, reverse=False, name="nsa_sel_decode")
    wb = win_buf.shape[2]
    win2 = win_buf.reshape(win_buf.shape[0] * B, wb * slots, HEAD_DIM)
    o_n = pl.pallas_call(
        functools.partial(_nsa_win_decode_kernel, G=G, wb=wb, past_len=past_len),
        out_shape=row_shape,
        grid=(B,),
        in_specs=[row_spec, row_spec,
                  pl.BlockSpec((None, wb * slots, HEAD_DIM), lambda b: (layer * B + b, 0, 0)),
                  row_spec, row_spec, row_spec],
        out_specs=row_spec,
        compiler_params=_cparams(1),
        name="nsa_win_decode",
    )(q3, kvn3, win2, o_c, o_s, g3)
    return o_n[:, :heads].reshape(B, heads * HEAD_DIM)


def _mem_decode_kernel(q_ref, kv_ref, o_ref, *, n_mem):
    scale = HEAD_DIM ** -0.5
    slots = 2 * MEM_H
    kcat = _gather_cat(kv_ref, 0, MEM_H, n_mem, slots)
    vcat = _gather_cat(kv_ref, MEM_H, MEM_H, n_mem, slots)
    s = _dot_nt(_block_diag(q_ref[...], MEM_H, 1).astype(BF16), kcat) * scale
    e = jnp.exp(s - jnp.max(s, axis=1, keepdims=True))
    p = e / jnp.sum(e, axis=1, keepdims=True)
    o_ref[...] = _diag_part(_dot(p.astype(BF16), vcat), MEM_H, 1)


def _mem_decode(qz, cache_mem, layer):
    B = qz.shape[0]
    n_mem = cache_mem.shape[2]
    slots = 2 * MEM_H
    E = MEM_H * HEAD_DIM
    q3 = _pad_rows(qz[:, :E].reshape(B, MEM_H, HEAD_DIM))
    kv2 = cache_mem.reshape(cache_mem.shape[0] * B, n_mem * slots, HEAD_DIM)
    row_spec = pl.BlockSpec((None, ROWS_PAD, HEAD_DIM), lambda b: (b, 0, 0))
    o = pl.pallas_call(
        functools.partial(_mem_decode_kernel, n_mem=n_mem),
        out_shape=jax.ShapeDtypeStruct((B, ROWS_PAD, HEAD_DIM), F32),
        grid=(B,),
        in_specs=[row_spec, pl.BlockSpec((None, n_mem * slots, HEAD_DIM), lambda b: (layer * B + b, 0, 0))],
        out_specs=row_spec,
        compiler_params=_cparams(1),
        name="mem_decode",
    )(q3, kv2)
    return o[:, :MEM_H].reshape(B, E)


TM_PROJ = 1024
TN_PROJ = 512
TM_OUT = 512


def _even_weights(w_in, w_out, cmp_w1, cmp_w2):
    G = cmp_w1.shape[1]
    hd = HEAD_DIM
    nq = G * NSA_GROUP * hd
    sizes = (nq, 6 * G * hd, 3 * G * NSA_GROUP, nq)
    c0, c1, c2, c3 = (sum(sizes[:k + 1]) for k in range(4))
    n_rest = w_in.shape[1] - c3
    sb_w = n_rest // 4
    gate = w_in[:, c1:c2].reshape(-1, G, NSA_GROUP * 3)
    gate_grouped = jnp.pad(gate, ((0, 0), (0, 0), (0, LANES - NSA_GROUP * 3))).reshape(-1, G * LANES)
    gate_flat = jnp.pad(w_in[:, c1:c2], ((0, 0), (0, LANES - (c2 - c1))))
    return dict(
        q_n=w_in[:, :c0].astype(BF16),
        kv_n=w_in[:, c0:c1].astype(BF16),
        gate_grouped=gate_grouped.astype(BF16),
        gate_flat=gate_flat.astype(BF16),
        z=jnp.concatenate([w_in[:, c2:c3], w_in[:, c3 + 3 * sb_w:]], axis=1).astype(BF16),
        sb_q=w_in[:, c3:c3 + sb_w].astype(BF16),
        sb_kv=w_in[:, c3 + sb_w:c3 + 3 * sb_w].astype(BF16),
        out=w_out.astype(BF16),
        w1=cmp_w1.astype(BF16),
        w2=cmp_w2.astype(BF16),
        G=G, sb_h=sb_w // hd,
    )


def _odd_weights(w_in, b_f, w_out):
    n_h = b_f.shape[0]
    c1 = 3 * n_h * HEAD_DIM
    return dict(
        q=w_in[:, :c1 // 3].astype(BF16),
        kv=w_in[:, c1 // 3:c1].astype(BF16),
        f=jnp.pad(w_in[:, c1:c1 + n_h], ((0, 0), (0, LANES - n_h))).astype(BF16),
        b_f=jnp.pad(b_f, (0, LANES - n_h)),
        z=w_in[:, c1 + n_h:].astype(BF16),
        out=w_out.astype(BF16),
        n_h=n_h,
    )


def _proj(x, g, w, rope=None, rope_heads=(), **kw):
    tn = TN_PROJ if w.shape[1] % TN_PROJ == 0 else min(w.shape[1], 256)
    return _norm_proj(x, g, w, tm=TM_PROJ, tn=tn, rope=rope, rope_heads=rope_heads, **kw)


def _even_project(x, g, W, rope):
    q_n = _proj(x, g, W["q_n"], rope, (1, 1, 1, 1))
    kv_n = _proj(x, g, W["kv_n"], rope, (1, 1, 0, 0))
    z = _proj(x, g, W["z"])
    sb_q = _proj(x, g, W["sb_q"])
    sb_kv = _proj(x, g, W["sb_kv"])
    return q_n, kv_n, z, sb_q, sb_kv


def _mem_block(xp, xs, i, mem_prompt, cache_mem, mx_norm, mx_mem_norm, mx_w_qz, mx_w_kv, mx_w_o):
    w_qz = mx_w_qz[i].astype(BF16)
    w_o = mx_w_o[i].astype(BF16)
    kv_mem = _proj(mem_prompt, mx_mem_norm[i], mx_w_kv[i].astype(BF16))
    qz = _proj(xp, mx_norm[i], w_qz)
    xp = _gated_out([_mem_attention(qz, kv_mem)], qz, 1, w_o, xp, tm=TM_OUT, tn=TN_PROJ)
    qz = _proj(xs, mx_norm[i], w_qz)
    xs = _gated_out([_mem_decode(qz, cache_mem, i)], qz, 1, w_o, xs, tm=TM_OUT, tn=TN_PROJ)
    return xp, xs, kv_mem


def kernel(x_prompt, x_sample, cache_nsa_cmp, cache_nsa_sel, cache_nsa_win, cache_sb, cache_fox_kv, cache_fox_logf, cache_mem, page_table, mem_prompt, ev_norm, ev_w_in, ev_cmp_pe, ev_cmp_w1, ev_cmp_w2, ev_w_out, od_norm, od_w_in, od_b_f, od_w_out, mx_norm, mx_mem_norm, mx_w_qz, mx_w_kv, mx_w_o, final_norm):
    bp, T, D = x_prompt.shape
    B, t_s, _ = x_sample.shape
    assert bp == 1 and t_s == 1
    depth = mx_norm.shape[0]
    n_pages = page_table.shape[1]
    past_len = n_pages * PAGE_SIZE
    rope_p = _rope_tables(jnp.arange(T, dtype=jnp.int32))
    rope_s = _rope_tables(jnp.full((B,), past_len, jnp.int32))
    xp = x_prompt.reshape(T, D)
    xs = x_sample.reshape(B, D)
    mem = mem_prompt.reshape(mem_prompt.shape[1], D)
    outs = {k: [] for k in ("cmp_p", "cmp_s", "sel_p", "sel_s", "win_p", "win_s", "sb_p", "sb_s",
                            "fkv_p", "fkv_s", "flf_p", "flf_s", "mem_p")}
    for i in range(depth):
        if i % 2 == 0:
            e = i // 2
            W = _even_weights(ev_w_in[e], ev_w_out[e], ev_cmp_w1[e], ev_cmp_w2[e])
            G, sb_h = W["G"], W["sb_h"]
            bw = 2 * G * HEAD_DIM
            w1_chunks = W["w1"].reshape(2, G, CMP_BLOCK, HEAD_DIM, HEAD_DIM)
            q_n, kv_n, z, sb_q, sb_kv = _even_project(xp, ev_norm[e], W, rope_p)
            gates = _proj(xp, ev_norm[e], W["gate_grouped"], act="sigmoid")
            kc_vc = _nsa_compress(kv_n, ev_cmp_pe[e], w1_chunks, W["w2"])
            o_n = _nsa_prompt(q_n, kv_n, gates, kc_vc)
            o_s = _sb_attention(sb_q, sb_kv, sb_h)
            xp = _gated_out([o_n, o_s], z, 0, W["out"], xp, tm=TM_OUT, tn=TN_PROJ)
            wbp = min(WINDOW, T)
            outs["cmp_p"].append(kv_n[:, :bw].reshape(1, T, 2, G, HEAD_DIM))
            outs["sel_p"].append(kv_n[:, bw:2 * bw].reshape(1, T, 2, G, HEAD_DIM))
            outs["win_p"].append(kv_n[T - wbp:, 2 * bw:].reshape(1, wbp, 2, G, HEAD_DIM))
            outs["sb_p"].append(sb_kv.reshape(1, T, 2, sb_h, HEAD_DIM))
            q_n, kv_n, z, sb_q, sb_kv = _even_project(xs, ev_norm[e], W, rope_s)
            gates = _proj(xs, ev_norm[e], W["gate_flat"], act="sigmoid")[:, :3 * G * NSA_GROUP]
            o_n = _nsa_decode(q_n, kv_n, gates, cache_nsa_cmp, cache_nsa_sel, cache_nsa_win, page_table, e,
                              ev_cmp_pe[e], W["w1"], W["w2"])
            o_s = _sb_decode(sb_q, cache_sb, page_table, e)
            xs = _gated_out([o_n, o_s], z, 0, W["out"], xs, tm=TM_OUT, tn=TN_PROJ)
            kv5 = kv_n.reshape(B, 1, 3, 2, G, HEAD_DIM)
            outs["cmp_s"].append(kv5[:, :, 0])
            outs["sel_s"].append(kv5[:, :, 1])
            outs["win_s"].append(jnp.concatenate([cache_nsa_win[e][:, t_s:], kv5[:, :, 2]], axis=1))
            outs["sb_s"].append(sb_kv.reshape(B, 1, 2, sb_h, HEAD_DIM))
        else:
            o = i // 2
            W = _odd_weights(od_w_in[o], od_b_f[o], od_w_out[o])
            n_h = W["n_h"]
            q = _proj(xp, od_norm[o], W["q"])
            kv = _proj(xp, od_norm[o], W["kv"])
            z = _proj(xp, od_norm[o], W["z"])
            logf = _proj(xp, od_norm[o], W["f"], bias=W["b_f"], act="logsigmoid")
            f_cum = _cumsum_rows(logf)[:, :n_h]
            xp = _gated_out([_fox_attention(q, kv, f_cum, n_h)], z, 0, W["out"], xp, tm=TM_OUT, tn=TN_PROJ)
            outs["fkv_p"].append(kv.reshape(1, T, 2, n_h, HEAD_DIM))
            outs["flf_p"].append(logf[:, :n_h].reshape(1, T, n_h))
            q = _proj(xs, od_norm[o], W["q"])
            kv = _proj(xs, od_norm[o], W["kv"])
            z = _proj(xs, od_norm[o], W["z"])
            logf = _proj(xs, od_norm[o], W["f"], bias=W["b_f"], act="logsigmoid")[:, :n_h]
            att = _fox_decode(q, kv, logf, cache_fox_kv, cache_fox_logf, page_table, o)
            xs = _gated_out([att], z, 0, W["out"], xs, tm=TM_OUT, tn=TN_PROJ)
            outs["fkv_s"].append(kv.reshape(B, 1, 2, n_h, HEAD_DIM))
            outs["flf_s"].append(logf.reshape(B, 1, n_h))
        xp, xs, kv_mem = _mem_block(xp, xs, i, mem, cache_mem, mx_norm, mx_mem_norm, mx_w_qz, mx_w_kv, mx_w_o)
        outs["mem_p"].append(kv_mem.reshape(1, mem.shape[0], 2, MEM_H, HEAD_DIM))
    y_prompt = _rmsnorm(xp, final_norm, tm=TM_OUT).reshape(1, T, D)
    y_sample = _rmsnorm(xs, final_norm, tm=TM_OUT).reshape(B, 1, D)
    return (y_prompt, y_sample) + tuple(
        jnp.stack(outs[k]) for k in ("cmp_p", "cmp_s", "sel_p", "sel_s", "win_p", "win_s", "sb_p", "sb_s",
                                     "fkv_p", "fkv_s", "flf_p", "flf_s", "mem_p"))
```

```python
import functools
import math

import jax
import jax.numpy as jnp
from jax import lax
from jax.experimental import pallas as pl
from jax.experimental.pallas import tpu as pltpu

F32 = jnp.float32
BF16 = jnp.bfloat16

HEAD_DIM = 128
PAGE_SIZE = 128
NSA_GROUP = 4
CMP_BLOCK = 32
CMP_STRIDE = 16
SEL_BLOCK = 64
SEL_TOPK = 16
WINDOW = 512
MEM_H = 4
ROPE_THETA = 10000.0
EPS = 1e-6
NEG = -1e30
SB_DEAD = -150.0

LANES = 128
VMEM_LIMIT = 52 * 1024 * 1024


def _cparams(n_axes):
    return pltpu.CompilerParams(dimension_semantics=("arbitrary",) * n_axes,
                                vmem_limit_bytes=VMEM_LIMIT)


def _sigmoid(x):
    return 1.0 / (1.0 + jnp.exp(-x))


def _softplus(x):
    return jnp.maximum(x, 0.0) + jnp.log(1.0 + jnp.exp(-jnp.abs(x)))


def _dot(a, b):
    return jnp.dot(a, b, preferred_element_type=F32)


def _dot_nt(a, b):
    return lax.dot_general(a, b, (((1,), (1,)), ((), ())), preferred_element_type=F32)


def _split3(x):
    h1 = x.astype(BF16)
    r1 = x - h1.astype(F32)
    h2 = r1.astype(BF16)
    h3 = (r1 - h2.astype(F32)).astype(BF16)
    return h1, h2, h3


def _dot_exact01(x, m01, parts=3):
    hs = _split3(x)[:parts]
    m = x.shape[0]
    if m % 16 == 0 and m <= 64:
        r = _dot(jnp.concatenate(hs, axis=0), m01)
        acc = r[0:m]
        for a in range(1, parts):
            acc = acc + r[a * m:(a + 1) * m]
        return acc
    acc = _dot(hs[0], m01)
    for h in hs[1:]:
        acc = acc + _dot(h, m01)
    return acc


def _rope_tables(pos):
    half = HEAD_DIM // 2
    inv = ROPE_THETA ** (-jnp.arange(half, dtype=F32) / half)
    ang = pos.astype(F32)[:, None] * inv[None, :]
    cos = jnp.cos(ang)
    sin = jnp.sin(ang)
    return jnp.concatenate([cos, cos], axis=1), jnp.concatenate([-sin, sin], axis=1)


def _proj_kernel(*refs, rope_heads, has_bias, act):
    x_ref, g_ref, w_ref = refs[:3]
    k = 3
    if any(rope_heads):
        cos_ref, sin_ref = refs[k], refs[k + 1]
        k += 2
    if has_bias:
        b_ref = refs[k]
        k += 1
    o_ref, xn_ref = refs[k], refs[k + 1]

    @pl.when(pl.program_id(1) == 0)
    def _():
        x = x_ref[...]
        r = lax.rsqrt(jnp.mean(x * x, axis=-1, keepdims=True) + EPS)
        xn_ref[...] = ((x * r) * g_ref[...]).astype(BF16)

    acc = _dot(xn_ref[...], w_ref[...])
    if has_bias:
        acc = acc + b_ref[...]
    if act == "sigmoid":
        acc = _sigmoid(acc)
    elif act == "logsigmoid":
        acc = -_softplus(-acc)
    if any(rope_heads):
        cos2 = cos_ref[...]
        sin2 = sin_ref[...]
        for hh, flag in enumerate(rope_heads):
            sl = slice(hh * HEAD_DIM, (hh + 1) * HEAD_DIM)
            xh = acc[:, sl]
            if flag:
                xh = xh * cos2 + pltpu.roll(xh, HEAD_DIM // 2, axis=1) * sin2
            o_ref[:, sl] = xh
    else:
        o_ref[...] = acc


def _norm_proj(x, g, w, *, tm, tn, rope=None, rope_heads=(), bias=None, act=None):
    M, D = x.shape
    N = w.shape[1]
    tm = min(tm, M)
    tn = min(tn, N)
    assert M % tm == 0 and N % tn == 0
    in_specs = [pl.BlockSpec((tm, D), lambda i, j: (i, 0)),
                pl.BlockSpec((1, D), lambda i, j: (0, 0)),
                pl.BlockSpec((D, tn), lambda i, j: (0, j))]
    args = [x, g.reshape(1, D), w]
    if any(rope_heads):
        assert len(rope_heads) * HEAD_DIM == tn
        in_specs += [pl.BlockSpec((tm, HEAD_DIM), lambda i, j: (i, 0))] * 2
        args += list(rope)
    if bias is not None:
        in_specs.append(pl.BlockSpec((1, tn), lambda i, j: (0, j)))
        args.append(bias.reshape(1, N))
    return pl.pallas_call(
        functools.partial(_proj_kernel, rope_heads=tuple(rope_heads), has_bias=bias is not None, act=act),
        out_shape=jax.ShapeDtypeStruct((M, N), F32),
        grid=(M // tm, N // tn),
        in_specs=in_specs,
        out_specs=pl.BlockSpec((tm, tn), lambda i, j: (i, j)),
        scratch_shapes=[pltpu.VMEM((tm, D), BF16)],
        compiler_params=_cparams(2),
        name="norm_proj",
    )(*args)


def _out_kernel(*refs, widths):
    n = len(widths)
    o_refs = refs[:n]
    z_ref, w_ref, res_ref, y_ref, mix_ref = refs[n:n + 5]

    @pl.when(pl.program_id(1) == 0)
    def _():
        off = 0
        for o_ref, wd in zip(o_refs, widths):
            z = z_ref[:, off:off + wd]
            mix_ref[:, off:off + wd] = (o_ref[...] * (z * _sigmoid(z))).astype(BF16)
            off += wd

    y_ref[...] = res_ref[...] + _dot(mix_ref[...], w_ref[...])


def _gated_out(o_parts, z, z_block, w, res, *, tm, tn):
    M, N = res.shape
    widths = tuple(o.shape[1] for o in o_parts)
    E = sum(widths)
    assert w.shape == (E, N)
    tm = min(tm, M)
    tn = min(tn, N)
    assert M % tm == 0 and N % tn == 0
    in_specs = [pl.BlockSpec((tm, wd), lambda i, j: (i, 0)) for wd in widths]
    in_specs += [pl.BlockSpec((tm, E), lambda i, j: (i, z_block)),
                 pl.BlockSpec((E, tn), lambda i, j: (0, j)),
                 pl.BlockSpec((tm, tn), lambda i, j: (i, j))]
    return pl.pallas_call(
        functools.partial(_out_kernel, widths=widths),
        out_shape=jax.ShapeDtypeStruct((M, N), F32),
        grid=(M // tm, N // tn),
        in_specs=in_specs,
        out_specs=pl.BlockSpec((tm, tn), lambda i, j: (i, j)),
        scratch_shapes=[pltpu.VMEM((tm, E), BF16)],
        compiler_params=_cparams(2),
        name="gated_out",
    )(*o_parts, z, w, res)


def _norm_kernel(x_ref, g_ref, o_ref):
    x = x_ref[...]
    r = lax.rsqrt(jnp.mean(x * x, axis=-1, keepdims=True) + EPS)
    o_ref[...] = (x * r) * g_ref[...]


def _rmsnorm(x, g, *, tm):
    M, D = x.shape
    tm = min(tm, M)
    return pl.pallas_call(
        _norm_kernel,
        out_shape=jax.ShapeDtypeStruct((M, D), F32),
        grid=(M // tm,),
        in_specs=[pl.BlockSpec((tm, D), lambda i: (i, 0)), pl.BlockSpec((1, D), lambda i: (0, 0))],
        out_specs=pl.BlockSpec((tm, D), lambda i: (i, 0)),
        compiler_params=_cparams(1),
        name="final_norm",
    )(x, g.reshape(1, D))


def _cumsum_kernel(x_ref, o_ref, carry_ref, *, tt):
    @pl.when(pl.program_id(0) == 0)
    def _():
        carry_ref[...] = jnp.zeros_like(carry_ref)

    r = lax.broadcasted_iota(jnp.int32, (tt, tt), 0)
    c = lax.broadcasted_iota(jnp.int32, (tt, tt), 1)
    tri = jnp.where(c <= r, 1.0, 0.0).astype(BF16)
    x = x_ref[...]
    h1, h2, h3 = _split3(x)
    cs = _dot(tri, h1) + _dot(tri, h2) + _dot(tri, h3) + carry_ref[...]
    o_ref[...] = cs
    carry_ref[...] = cs[tt - 1:tt, :]


def _cumsum_rows(x, *, tt=256):
    T, N = x.shape
    tt = min(tt, T)
    assert T % tt == 0
    return pl.pallas_call(
        functools.partial(_cumsum_kernel, tt=tt),
        out_shape=jax.ShapeDtypeStruct((T, N), F32),
        grid=(T // tt,),
        in_specs=[pl.BlockSpec((tt, N), lambda i: (i, 0))],
        out_specs=pl.BlockSpec((tt, N), lambda i: (i, 0)),
        scratch_shapes=[pltpu.VMEM((1, N), F32)],
        compiler_params=_cparams(1),
        name="cumsum_rows",
    )(x)


def _fox_kernel(q_ref, k_ref, v_ref, fq_ref, fk_ref, o_ref, *, tq, tk):
    i = pl.program_id(1)
    scale = HEAD_DIM ** -0.5
    q = q_ref[...].astype(BF16)
    fq = jnp.concatenate([fq_ref[...]] * (tk // LANES), axis=1)

    def tile(kt, carry, masked):
        m, l, acc = carry
        start = pl.multiple_of(kt * tk, tk)
        k = k_ref[pl.ds(start, tk), :].astype(BF16)
        v = v_ref[pl.ds(start, tk), :].astype(BF16)
        s = _dot_nt(q, k) * scale
        s = s + fq - fk_ref[:, pl.ds(start, tk)]
        if masked:
            qpos = i * tq + lax.broadcasted_iota(jnp.int32, (tq, tk), 0)
            kpos = start + lax.broadcasted_iota(jnp.int32, (tq, tk), 1)
            valid = kpos <= qpos
            s = jnp.where(valid, s, NEG)
        m_new = jnp.maximum(m, jnp.max(s, axis=1, keepdims=True))
        e = jnp.exp(s - m_new)
        if masked:
            e = jnp.where(valid, e, 0.0)
        alpha = jnp.exp(m - m_new)
        l = alpha * l + jnp.sum(e, axis=1, keepdims=True)
        acc = alpha * acc + _dot(e.astype(BF16), v)
        return m_new, l, acc

    init = (jnp.full((tq, 1), NEG, F32), jnp.zeros((tq, 1), F32), jnp.zeros((tq, HEAD_DIM), F32))
    n_full = (i * tq) // tk
    carry = lax.fori_loop(0, n_full, lambda kt, c: tile(kt, c, False), init)
    for d in range(tq // tk):
        carry = tile(n_full + d, carry, True)
    m, l, acc = carry
    o_ref[...] = acc / jnp.maximum(l, 1e-30)


def _fox_attention(q, kv, f_cum, n_heads, *, tq=512, tk=512):
    T = q.shape[0]
    tq = min(tq, T)
    tk = min(tk, tq)
    assert T % tq == 0 and tq % tk == 0
    f_t = f_cum.T
    f_row = f_t.reshape(n_heads, 1, T)
    f_rep = jnp.broadcast_to(f_t[:, :, None], (n_heads, T, LANES))
    return pl.pallas_call(
        functools.partial(_fox_kernel, tq=tq, tk=tk),
        out_shape=jax.ShapeDtypeStruct((T, n_heads * HEAD_DIM), F32),
        grid=(n_heads, T // tq),
        in_specs=[pl.BlockSpec((tq, HEAD_DIM), lambda h, i: (i, h)),
                  pl.BlockSpec((T, HEAD_DIM), lambda h, i: (0, h)),
                  pl.BlockSpec((T, HEAD_DIM), lambda h, i: (0, n_heads + h)),
                  pl.BlockSpec((None, tq, LANES), lambda h, i: (h, i, 0)),
                  pl.BlockSpec((None, 1, T), lambda h, i: (h, 0, 0))],
        out_specs=pl.BlockSpec((tq, HEAD_DIM), lambda h, i: (i, h)),
        compiler_params=_cparams(2),
        name="fox_attention",
    )(q, kv, kv, f_rep, f_row)


def _sb_kernel(q_ref, k_ref, v_ref, o_ref, *, tq):
    i = pl.program_id(1)
    tk = tq
    scale = HEAD_DIM ** -0.5
    q = q_ref[...].astype(BF16)
    r = lax.broadcasted_iota(jnp.int32, (LANES, LANES), 0)
    c = lax.broadcasted_iota(jnp.int32, (LANES, LANES), 1)
    suffix = jnp.where(r > c, 1.0, 0.0).astype(BF16)

    def tile(kt, c_in, acc, masked):
        start = pl.multiple_of(kt * tk, tk)
        k = k_ref[pl.ds(start, tk), :].astype(BF16)
        v = v_ref[pl.ds(start, tk), :].astype(BF16)
        z = _dot_nt(q, k) * scale
        sp = _softplus(z)
        log_keep = -sp
        if masked:
            valid = lax.broadcasted_iota(jnp.int32, (tq, tk), 1) < lax.broadcasted_iota(jnp.int32, (tq, tk), 0)
            log_keep = jnp.where(valid, log_keep, 0.0)
        carry = c_in
        after = [None] * (tk // LANES)
        for t in reversed(range(tk // LANES)):
            chunk = log_keep[:, t * LANES:(t + 1) * LANES]
            after[t] = _dot_exact01(chunk, suffix, parts=2) + carry
            carry = carry + jnp.sum(chunk, axis=1, keepdims=True)
        a = jnp.exp((z - sp) + (jnp.concatenate(after, axis=1) if len(after) > 1 else after[0]))
        if masked:
            a = jnp.where(valid, a, 0.0)
        acc = acc + _dot(a.astype(BF16), v)
        return carry, acc

    c0, acc0 = tile(i, jnp.zeros((tq, LANES), F32), jnp.zeros((tq, HEAD_DIM), F32), True)

    def cond(st):
        kt, alive, _, _ = st
        return jnp.logical_and(kt >= 0, alive)

    def body(st):
        kt, _, c_in, acc = st
        c_out, acc = tile(kt, c_in, acc, False)
        return kt - 1, jnp.max(c_out) > SB_DEAD, c_out, acc

    _, _, _, acc = lax.while_loop(cond, body, (i - 1, jnp.max(c0) > SB_DEAD, c0, acc0))
    o_ref[...] = acc


def _sb_attention(q, kv, n_heads, *, tq=256):
    T = q.shape[0]
    tq = min(tq, T)
    assert T % tq == 0
    return pl.pallas_call(
        functools.partial(_sb_kernel, tq=tq),
        out_shape=jax.ShapeDtypeStruct((T, n_heads * HEAD_DIM), F32),
        grid=(n_heads, T // tq),
        in_specs=[pl.BlockSpec((tq, HEAD_DIM), lambda h, i: (i, h)),
                  pl.BlockSpec((T, HEAD_DIM), lambda h, i: (0, h)),
                  pl.BlockSpec((T, HEAD_DIM), lambda h, i: (0, n_heads + h))],
        out_specs=pl.BlockSpec((tq, HEAD_DIM), lambda h, i: (i, h)),
        compiler_params=_cparams(2),
        name="sb_attention",
    )(q, kv, kv)


N_CMP_PARTS = CMP_BLOCK // CMP_STRIDE
SEL_PER_CMP = SEL_BLOCK // CMP_STRIDE
SEL_TILE = 512


def _compress_mlp(chunk_sums, w2_ref, o_ref):
    n = chunk_sums[0].shape[0]
    h = chunk_sums[0]
    for p in range(1, N_CMP_PARTS):
        h = h + pltpu.roll(chunk_sums[p], n - p, axis=0)
    h = h * _sigmoid(h)
    o_ref[...] = _dot(h.astype(BF16), w2_ref[...])


def _compress_kernel(rows_ref, pe_ref, w1_ref, w2_ref, o_ref, *, n_chunks):
    sums = []
    for p in range(N_CMP_PARTS):
        acc = jnp.zeros((n_chunks, HEAD_DIM), F32)
        for j in range(CMP_STRIDE):
            jj = p * CMP_STRIDE + j
            x = rows_ref[pl.ds(j, n_chunks, stride=CMP_STRIDE), :]
            acc = acc + _dot((x + pe_ref[jj:jj + 1, :]).astype(BF16), w1_ref[jj])
        sums.append(acc)
    _compress_mlp(sums, w2_ref, o_ref)


def _nsa_compress(kv_n, pe, w1, w2):
    T = kv_n.shape[0]
    G = w1.shape[1]
    n_chunks = T // CMP_STRIDE
    return pl.pallas_call(
        functools.partial(_compress_kernel, n_chunks=n_chunks),
        out_shape=jax.ShapeDtypeStruct((2, G, n_chunks, HEAD_DIM), F32),
        grid=(2, G),
        in_specs=[pl.BlockSpec((T, HEAD_DIM), lambda a, g: (0, a * G + g)),
                  pl.BlockSpec((None, CMP_BLOCK, HEAD_DIM), lambda a, g: (a, 0, 0)),
                  pl.BlockSpec((None, None, CMP_BLOCK, HEAD_DIM, HEAD_DIM), lambda a, g: (a, g, 0, 0, 0)),
                  pl.BlockSpec((None, None, HEAD_DIM, HEAD_DIM), lambda a, g: (a, g, 0, 0))],
        out_specs=pl.BlockSpec((None, None, n_chunks, HEAD_DIM), lambda a, g: (a, g, 0, 0)),
        compiler_params=_cparams(2),
        name="nsa_compress",
    )(kv_n, pe, w1, w2)


def _stack_heads(q):
    r = q.shape[1] // HEAD_DIM
    return jnp.concatenate([q[:, a * HEAD_DIM:(a + 1) * HEAD_DIM] for a in range(r)], axis=0)


def _unstack_heads(o_ref, o, tq):
    for a in range(o.shape[0] // tq):
        o_ref[:, a * HEAD_DIM:(a + 1) * HEAD_DIM] = o[a * tq:(a + 1) * tq, :]


def _select_blocks(imp_sel, qpos, ns, nsp):
    rows = imp_sel.shape[0]
    blk_i = lax.broadcasted_iota(jnp.int32, (rows, nsp), 1)
    blk = blk_i.astype(F32)
    visible = blk_i * SEL_BLOCK <= qpos
    is_cur = blk_i == qpos // SEL_BLOCK
    score = jnp.where(is_cur, 1e9, jnp.where(visible, imp_sel, -1e9))
    score = jnp.where(blk_i < ns, score, -3e38)

    def pick(_, st):
        score, sel = st
        m = jnp.max(score, axis=1, keepdims=True)
        idx = jnp.min(jnp.where(score == m, blk, float(nsp)), axis=1, keepdims=True)
        hit = blk == idx
        sel = jnp.where(hit & (m > -1e8), 1.0, sel)
        return jnp.where(hit, -3e38, score), sel

    _, sel = lax.fori_loop(0, min(SEL_TOPK, ns), pick, (score, jnp.zeros((rows, nsp), F32)))
    return sel


def _cmp_to_sel_matrix(ncp, nsp, nc):
    c = lax.broadcasted_iota(jnp.int32, (ncp, nsp), 0)
    j = lax.broadcasted_iota(jnp.int32, (ncp, nsp), 1)
    lo = SEL_PER_CMP * j - (N_CMP_PARTS - 1)
    hit = (c >= lo) & (c <= lo + SEL_PER_CMP + N_CMP_PARTS - 2) & (c < nc)
    return jnp.where(hit, 1.0, 0.0).astype(BF16)


def _nsa_cmp_kernel(q_ref, kcvc_ref, oc_ref, sel_ref, *, tq, nc, ns):
    i = pl.program_id(0)
    G = kcvc_ref.shape[1]
    ncp = kcvc_ref.shape[2]
    nsp = sel_ref.shape[2]
    gw = q_ref.shape[1] // G
    scale = HEAD_DIM ** -0.5
    pooling = _cmp_to_sel_matrix(ncp, nsp, nc)
    imp_sel = []
    for g in range(G):
        q = _stack_heads(q_ref[:, g * gw:(g + 1) * gw]).astype(BF16)
        rows = q.shape[0]
        s = _dot_nt(q, kcvc_ref[0, g].astype(BF16)) * scale
        qpos = i * tq + (lax.broadcasted_iota(jnp.int32, (rows, ncp), 0) & (tq - 1))
        cidx = lax.broadcasted_iota(jnp.int32, (rows, ncp), 1)
        valid = (cidx * CMP_STRIDE + (CMP_BLOCK - 1) <= qpos) & (cidx < nc)
        s = jnp.where(valid, s, NEG)
        e = jnp.where(valid, jnp.exp(s - jnp.max(s, axis=1, keepdims=True)), 0.0)
        p = e / jnp.maximum(jnp.sum(e, axis=1, keepdims=True), 1e-30)
        _unstack_heads(oc_ref.at[:, g * gw:(g + 1) * gw], _dot(p.astype(BF16), kcvc_ref[1, g].astype(BF16)), tq)
        imp = p[0:tq]
        for a in range(1, rows // tq):
            imp = imp + p[a * tq:(a + 1) * tq]
        imp_sel.append(_dot_exact01(imp, pooling))
    imp_all = jnp.concatenate(imp_sel, axis=0)
    qpos1 = i * tq + (lax.broadcasted_iota(jnp.int32, imp_all.shape, 0) & (tq - 1))
    sel = _select_blocks(imp_all, qpos1, ns, nsp).astype(BF16)
    for g in range(G):
        sel_ref[g] = sel[g * tq:(g + 1) * tq]


M_INIT = -1e20


def _nsa_sel_kernel(q_ref, k_ref, v_ref, sel_ref, o_ref, *, tq, tk):
    i = pl.program_id(1)
    nsp = sel_ref.shape[1]
    scale = HEAD_DIM ** -0.5
    q = _stack_heads(q_ref[...]).astype(BF16)
    heads = q.shape[0] // tq
    sel = sel_ref[...]
    jidx = lax.broadcasted_iota(jnp.int32, (nsp, tk), 0)
    koff = lax.broadcasted_iota(jnp.int32, (nsp, tk), 1) // SEL_BLOCK
    qpos = i * tq + lax.broadcasted_iota(jnp.int32, (tq, tk), 0)
    kidx = lax.broadcasted_iota(jnp.int32, (tq, tk), 1)

    def tile(kt, carry):
        m, l, acc = carry
        start = pl.multiple_of(kt * tk, tk)
        k = k_ref[pl.ds(start, tk), :].astype(BF16)
        v = v_ref[pl.ds(start, tk), :].astype(BF16)
        expand = jnp.where(jidx == kt * (tk // SEL_BLOCK) + koff, 1.0, 0.0).astype(BF16)
        chosen = _dot(sel, expand)
        bias1 = jnp.where((chosen > 0.5) & (start + kidx <= qpos), 0.0, NEG)
        s = _dot_nt(q, k) * scale + jnp.concatenate([bias1] * heads, axis=0)
        m_new = jnp.maximum(m, jnp.max(s, axis=1, keepdims=True))
        e = jnp.exp(s - m_new)
        alpha = jnp.exp(m - m_new)
        l = alpha * l + jnp.sum(e, axis=1, keepdims=True)
        acc = alpha * acc + _dot(e.astype(BF16), v)
        return m_new, l, acc

    rows = q.shape[0]
    init = (jnp.full((rows, 1), M_INIT, F32), jnp.zeros((rows, 1), F32), jnp.zeros((rows, HEAD_DIM), F32))
    _, l, acc = lax.fori_loop(0, ((i + 1) * tq + tk - 1) // tk, tile, init)
    _unstack_heads(o_ref, acc / jnp.maximum(l, 1e-30), tq)


def _nsa_win_kernel(q_ref, k_ref, v_ref, oc_ref, os_ref, gate_ref, o_ref, *, tq, wk):
    i = pl.program_id(1)
    scale = HEAD_DIM ** -0.5
    q = _stack_heads(q_ref[...]).astype(BF16)
    heads = q.shape[0] // tq
    n_keys = k_ref.shape[0]
    start = pl.multiple_of(jnp.clip(i * tq - WINDOW, 0, n_keys - wk), tq)
    k = k_ref[pl.ds(start, wk), :].astype(BF16)
    v = v_ref[pl.ds(start, wk), :].astype(BF16)
    qpos = i * tq + lax.broadcasted_iota(jnp.int32, (tq, wk), 0)
    kpos = start + lax.broadcasted_iota(jnp.int32, (tq, wk), 1)
    bias1 = jnp.where((kpos <= qpos) & (kpos > qpos - WINDOW), 0.0, NEG)
    s = _dot_nt(q, k) * scale + jnp.concatenate([bias1] * heads, axis=0)
    e = jnp.exp(s - jnp.max(s, axis=1, keepdims=True))
    l = jnp.sum(e, axis=1, keepdims=True)
    o_w = _dot(e.astype(BF16), v) / jnp.maximum(l, 1e-30)
    gates = gate_ref[...]
    for a in range(heads):
        sl = slice(a * HEAD_DIM, (a + 1) * HEAD_DIM)
        o_ref[:, sl] = (gates[:, 3 * a:3 * a + 1] * oc_ref[:, sl]
                        + gates[:, 3 * a + 1:3 * a + 2] * os_ref[:, sl]
                        + gates[:, 3 * a + 2:3 * a + 3] * o_w[a * tq:(a + 1) * tq, :])


def _nsa_prompt(q_n, kv_n, gates, kc_vc, *, tq=128):
    T = q_n.shape[0]
    G = kc_vc.shape[1]
    gw = q_n.shape[1] // G
    n_chunks = kc_vc.shape[2]
    nc = n_chunks - (N_CMP_PARTS - 1)
    ns = T // SEL_BLOCK
    nsp = -(-ns // LANES) * LANES
    tq = min(tq, T)
    nq = T // tq
    q_spec = pl.BlockSpec((tq, gw), lambda g, i: (i, g))
    o_shape = jax.ShapeDtypeStruct((T, G * gw), F32)

    def kv_spec(branch, a):
        return pl.BlockSpec((T, HEAD_DIM), lambda g, i: (0, (branch * 2 + a) * G + g))

    o_c, sel = pl.pallas_call(
        functools.partial(_nsa_cmp_kernel, tq=tq, nc=nc, ns=ns),
        out_shape=(o_shape, jax.ShapeDtypeStruct((G, T, nsp), BF16)),
        grid=(nq,),
        in_specs=[pl.BlockSpec((tq, G * gw), lambda i: (i, 0)),
                  pl.BlockSpec(kc_vc.shape, lambda i: (0, 0, 0, 0))],
        out_specs=(pl.BlockSpec((tq, G * gw), lambda i: (i, 0)), pl.BlockSpec((G, tq, nsp), lambda i: (0, i, 0))),
        compiler_params=_cparams(1),
        name="nsa_cmp",
    )(q_n, kc_vc)
    tk = min(SEL_TILE, T)
    assert T % tk == 0 and tk % SEL_BLOCK == 0
    o_s = pl.pallas_call(
        functools.partial(_nsa_sel_kernel, tq=tq, tk=tk),
        out_shape=o_shape,
        grid=(G, nq),
        in_specs=[q_spec, kv_spec(1, 0), kv_spec(1, 1), pl.BlockSpec((None, tq, nsp), lambda g, i: (g, i, 0))],
        out_specs=q_spec,
        compiler_params=_cparams(2),
        name="nsa_sel",
    )(q_n, kv_n, kv_n, sel)
    wk = min(WINDOW + tq, T)
    assert (T - wk) % tq == 0 and WINDOW % tq == 0
    return pl.pallas_call(
        functools.partial(_nsa_win_kernel, tq=tq, wk=wk),
        out_shape=o_shape,
        grid=(G, nq),
        in_specs=[q_spec, kv_spec(2, 0), kv_spec(2, 1), q_spec, q_spec,
                  pl.BlockSpec((tq, LANES), lambda g, i: (i, g))],
        out_specs=q_spec,
        compiler_params=_cparams(2),
        name="nsa_win",
    )(q_n, kv_n, kv_n, o_c, o_s, gates)


def _mem_kernel(q_ref, kv_ref, o_ref):
    scale = HEAD_DIM ** -0.5
    for h in range(MEM_H):
        sl = slice(h * HEAD_DIM, (h + 1) * HEAD_DIM)
        k = kv_ref[:, sl].astype(BF16)
        v = kv_ref[:, MEM_H * HEAD_DIM + h * HEAD_DIM:MEM_H * HEAD_DIM + (h + 1) * HEAD_DIM].astype(BF16)
        s = _dot_nt(q_ref[:, sl].astype(BF16), k) * scale
        e = jnp.exp(s - jnp.max(s, axis=1, keepdims=True))
        p = e / jnp.sum(e, axis=1, keepdims=True)
        o_ref[:, sl] = _dot(p.astype(BF16), v)


def _mem_attention(qz, kv, *, tq=512):
    T = qz.shape[0]
    M = kv.shape[0]
    E = MEM_H * HEAD_DIM
    tq = min(tq, T)
    return pl.pallas_call(
        _mem_kernel,
        out_shape=jax.ShapeDtypeStruct((T, E), F32),
        grid=(T // tq,),
        in_specs=[pl.BlockSpec((tq, E), lambda i: (i, 0)), pl.BlockSpec((M, 2 * E), lambda i: (0, 0))],
        out_specs=pl.BlockSpec((tq, E), lambda i: (i, 0)),
        compiler_params=_cparams(1),
        name="mem_attention",
    )(qz, kv)


ROWS_PAD = 16


def _head_of_lane(rows, hk):
    lane_head = lax.broadcasted_iota(jnp.int32, (rows, hk * HEAD_DIM), 1) // HEAD_DIM
    row = lax.broadcasted_iota(jnp.int32, (rows, hk * HEAD_DIM), 0)
    return lane_head, row


def _block_diag(x, hk, group):
    lane_head, row = _head_of_lane(x.shape[0], hk)
    return jnp.where(lane_head == row // group, jnp.concatenate([x] * hk, axis=1), 0.0)


def _diag_part(acc, hk, group):
    rows = acc.shape[0]
    row = lax.broadcasted_iota(jnp.int32, (rows, HEAD_DIM), 0)
    out = jnp.zeros((rows, HEAD_DIM), F32)
    for h in range(hk):
        out = jnp.where(row // group == h, acc[:, h * HEAD_DIM:(h + 1) * HEAD_DIM], out)
    return out


def _expand_rows(x, group, rows):
    parts = [jnp.broadcast_to(x[h:h + 1, :], (group, HEAD_DIM)) for h in range(x.shape[0])]
    pad = rows - group * x.shape[0]
    if pad:
        parts.append(jnp.zeros((pad, HEAD_DIM), F32))
    return jnp.concatenate(parts, axis=0) if len(parts) > 1 else parts[0]


def _gather_cat(ref, first, hk, n_keys, stride):
    return jnp.concatenate([ref[pl.ds(first + h, n_keys, stride=stride), :] for h in range(hk)],
                           axis=1).astype(BF16)


def _rep(x, hk):
    return jnp.concatenate([x] * hk, axis=1)


def _suffix_matrix(n):
    r = lax.broadcasted_iota(jnp.int32, (n, 2 * n), 0)
    c = lax.broadcasted_iota(jnp.int32, (n, 2 * n), 1)
    return jnp.where((r > c) | (c >= n), 1.0, 0.0).astype(BF16)


def _lane_rep(col, width=LANES):
    return jnp.broadcast_to(col, (col.shape[0], width))


SUBLANES = 8


def _own_head(rows, n_cols, hk):
    col = lax.broadcasted_iota(jnp.int32, (rows, n_cols), 1)
    row = lax.broadcasted_iota(jnp.int32, (rows, n_cols), 0)
    return ((col & (SUBLANES - 1)) == (row & (SUBLANES - 1))) & (row < hk)


def _page_scores(q, kv_ref, hk):
    qb = q.astype(BF16)
    row = lax.broadcasted_iota(jnp.int32, (ROWS_PAD, PAGE_SIZE * SUBLANES), 0)
    s = None
    for t in range(hk // SUBLANES):
        kt = kv_ref[:, t].reshape(PAGE_SIZE * SUBLANES, HEAD_DIM).astype(BF16)
        st = _dot_nt(qb, kt)
        s = st if s is None else jnp.where(row < t * SUBLANES, s, st)
    return s


def _page_values(w, kv_ref, hk):
    wb = w.astype(BF16)
    row = lax.broadcasted_iota(jnp.int32, (ROWS_PAD, HEAD_DIM), 0)
    o = None
    n_t = hk // SUBLANES
    for t in range(n_t):
        vt = kv_ref[:, n_t + t].reshape(PAGE_SIZE * SUBLANES, HEAD_DIM).astype(BF16)
        ot = _dot(wb, vt)
        o = ot if o is None else jnp.where(row < t * SUBLANES, o, ot)
    return o


def _fox_decode_kernel(pt_ref, q_ref, kvn_ref, lfn_ref, *rest, hk, pps):
    kv_refs, lf_refs = rest[:pps], rest[pps:2 * pps]
    u_ref, o_ref, m_ref, l_ref, c_ref, acc_ref = rest[2 * pps:]
    p = pl.program_id(1)
    scale = HEAD_DIM ** -0.5
    n_cols = PAGE_SIZE * SUBLANES
    q = q_ref[...]

    @pl.when(p == 0)
    def _():
        s_new = jnp.sum(q * kvn_ref[0:hk, :], axis=1, keepdims=True) * scale
        m_ref[...] = _lane_rep(s_new)
        l_ref[...] = jnp.ones_like(l_ref)
        c_ref[...] = lfn_ref[...]
        acc_ref[...] = kvn_ref[hk:2 * hk, :]

    keep = _own_head(ROWS_PAD, n_cols, hk)
    c = c_ref[...]
    scores = []
    for kv_ref, lf_ref in zip(kv_refs, lf_refs):
        lf = lf_ref[...]
        bias = _dot_exact01(lf, u_ref[...]) + _rep(c, n_cols // LANES)
        scores.append(jnp.where(keep, _page_scores(q, kv_ref, hk) * scale + bias, NEG))
        c = c + jnp.sum(lf, axis=1, keepdims=True)
    m_old = m_ref[...]
    m_new = m_old
    for s in scores:
        m_new = jnp.maximum(m_new, jnp.max(s, axis=1, keepdims=True))
    alpha = jnp.exp(m_old - m_new)
    l = alpha * l_ref[...]
    acc = alpha * acc_ref[...]
    for s, kv_ref in zip(scores, kv_refs):
        e = jnp.exp(s - m_new[:, :1])
        l = l + jnp.sum(e, axis=1, keepdims=True)
        acc = acc + _page_values(e, kv_ref, hk)
    l_ref[...] = l
    acc_ref[...] = acc
    m_ref[...] = m_new
    c_ref[...] = c

    @pl.when(p == pl.num_programs(1) - 1)
    def _():
        o_ref[...] = acc / jnp.maximum(l, 1e-30)


def _sb_decode_kernel(pt_ref, q_ref, *rest, hk, pps):
    kv_refs = rest[:pps]
    o_ref, c_ref, acc_ref = rest[pps:]
    p = pl.program_id(1)
    scale = HEAD_DIM ** -0.5
    n_cols = PAGE_SIZE * SUBLANES

    @pl.when(p == 0)
    def _():
        row = lax.broadcasted_iota(jnp.int32, c_ref.shape, 0)
        c_ref[...] = jnp.where(row < hk, 0.0, NEG)
        acc_ref[...] = jnp.zeros_like(acc_ref)

    for kv_ref in kv_refs:
        @pl.when(jnp.max(c_ref[...]) > SB_DEAD)
        def _(kv_ref=kv_ref):
            z = _page_scores(q_ref[...], kv_ref, hk) * scale
            keep = _own_head(ROWS_PAD, n_cols, hk)
            sp = _softplus(z)
            log_keep = jnp.where(keep, -sp, 0.0)
            r = lax.broadcasted_iota(jnp.int32, (LANES, LANES), 0)
            cc = lax.broadcasted_iota(jnp.int32, (LANES, LANES), 1)
            suffix = jnp.where(r > cc, 1.0, 0.0).astype(BF16)
            carry = c_ref[...]
            after = [None] * (n_cols // LANES)
            for t in reversed(range(n_cols // LANES)):
                chunk = log_keep[:, t * LANES:(t + 1) * LANES]
                after[t] = _dot_exact01(chunk, suffix, parts=2) + carry
                carry = carry + jnp.sum(chunk, axis=1, keepdims=True)
            a = jnp.where(keep, jnp.exp((z - sp) + jnp.concatenate(after, axis=1)), 0.0)
            acc_ref[...] = acc_ref[...] + _page_values(a, kv_ref, hk)
            c_ref[...] = carry

    @pl.when(p == pl.num_programs(1) - 1)
    def _():
        o_ref[...] = acc_ref[...]


def _paged_decode(kern, pools, page_table, layer, small_inputs, const_inputs, scratch, *, pps, reverse, name):
    B, n_pages = page_table.shape
    assert n_pages % pps == 0
    pt = page_table.reshape(B * n_pages)

    def page_spec(arr, n_pool, j):
        def index(b, p, pt_ref):
            pp = p * pps + j
            if reverse:
                pp = n_pages - 1 - pp
            return (layer * n_pool + pt_ref[b * n_pages + pp],) + (0,) * (arr.ndim - 1)
        return pl.BlockSpec((None,) + arr.shape[1:], index)

    in_specs = [pl.BlockSpec((None,) + x.shape[1:], lambda b, p, pt_ref: (b, 0, 0)) for x in small_inputs]
    args = list(small_inputs)
    for arr, n_pool in pools:
        for j in range(pps):
            in_specs.append(page_spec(arr, n_pool, j))
            args.append(arr)
    for x in const_inputs:
        in_specs.append(pl.BlockSpec(x.shape, lambda b, p, pt_ref, nd=x.ndim: (0,) * nd))
        args.append(x)
    return pl.pallas_call(
        kern,
        out_shape=jax.ShapeDtypeStruct((B, ROWS_PAD, HEAD_DIM), F32),
        grid_spec=pltpu.PrefetchScalarGridSpec(
            num_scalar_prefetch=1,
            grid=(B, n_pages // pps),
            in_specs=in_specs,
            out_specs=pl.BlockSpec((None, ROWS_PAD, HEAD_DIM), lambda b, p, pt_ref: (b, 0, 0)),
            scratch_shapes=scratch),
        compiler_params=_cparams(2),
        name=name,
    )(pt, *args)


def _pad_rows(x, rows=ROWS_PAD):
    return jnp.pad(x, ((0, 0), (0, rows - x.shape[1]), (0, 0)))


def _head_tiled_pool(pool):
    n_layers, n_pool, _, _, hk, _ = pool.shape
    assert hk % SUBLANES == 0
    return pool.reshape(n_layers * n_pool, PAGE_SIZE, 2 * hk // SUBLANES, SUBLANES, HEAD_DIM)


def _fox_decode(q, kv_new, logf_new, pool, logf_pool, page_table, layer, *, pps=2):
    B = q.shape[0]
    hk = pool.shape[4]
    assert hk == ROWS_PAD
    n_pool = pool.shape[1]
    lf_t = jnp.swapaxes(logf_pool, 2, 3).reshape(logf_pool.shape[0] * n_pool, hk, PAGE_SIZE)
    small = [q.reshape(B, hk, HEAD_DIM), kv_new.reshape(B, 2 * hk, HEAD_DIM),
             jnp.broadcast_to(logf_new[:, :, None], (B, hk, LANES))]
    n_cols = PAGE_SIZE * SUBLANES
    key_of_col = lax.broadcasted_iota(jnp.int32, (PAGE_SIZE, n_cols), 1) // SUBLANES
    u = (lax.broadcasted_iota(jnp.int32, (PAGE_SIZE, n_cols), 0) > key_of_col).astype(BF16)
    scratch = [pltpu.VMEM((ROWS_PAD, LANES), F32)] * 4
    o = _paged_decode(functools.partial(_fox_decode_kernel, hk=hk, pps=pps),
                      [(_head_tiled_pool(pool), n_pool), (lf_t, n_pool)], page_table, layer, small, [u], scratch,
                      pps=pps, reverse=True, name="fox_decode")
    return o.reshape(B, hk * HEAD_DIM)


def _sb_decode(q, pool, page_table, layer, *, pps=2):
    B = q.shape[0]
    hk = pool.shape[4]
    small = [_pad_rows(q.reshape(B, hk, HEAD_DIM))]
    scratch = [pltpu.VMEM((ROWS_PAD, LANES), F32)] * 2
    o = _paged_decode(functools.partial(_sb_decode_kernel, hk=hk, pps=pps), [(_head_tiled_pool(pool), pool.shape[1])],
                      page_table, layer, small, [], scratch, pps=pps, reverse=True, name="sb_decode")
    return o[:, :hk].reshape(B, hk * HEAD_DIM)


def _cmp_decode_kernel(pt_ref, *refs, G, nb, n_pages):
    page_refs = refs[:n_pages]
    pe_ref, w1_ref, w2_ref, o_ref, x_ref = refs[n_pages:]
    s = pl.program_id(1)
    cpp = PAGE_SIZE // CMP_STRIDE
    slots = 2 * G
    for p, pool_ref in enumerate(page_refs):
        base = pl.multiple_of((s * n_pages + p) * cpp, cpp)
        for a in range(2):
            for g in range(G):
                for j in range(CMP_STRIDE):
                    x_ref[a * G + g, pl.ds(base, cpp), j * HEAD_DIM:(j + 1) * HEAD_DIM] = (
                        pool_ref[pl.ds(j * slots + a * G + g, cpp, stride=CMP_STRIDE * slots), :])

    @pl.when(s == nb - 1)
    def _():
        for a in range(2):
            for g in range(G):
                x = x_ref[a * G + g]
                sums = [_dot((x + pe_ref[a, part:part + 1, :]).astype(BF16), w1_ref[a, g, part])
                        for part in range(N_CMP_PARTS)]
                _compress_mlp(sums, w2_ref.at[a, g], o_ref.at[a, g])


def _cmp_decode(pool, page_table, layer, pe, w1, w2, *, nb=4):
    n_pool = pool.shape[1]
    G = pool.shape[4]
    slots = 2 * G
    B, n_pages = page_table.shape
    nb = min(nb, B)
    assert B % nb == 0
    cpp = PAGE_SIZE // CMP_STRIDE
    rows = nb * n_pages * cpp
    pool2 = pool.reshape(pool.shape[0] * n_pool, PAGE_SIZE * slots, HEAD_DIM)
    pt = page_table.reshape(B * n_pages)
    pe2 = pe.reshape(2, N_CMP_PARTS, CMP_STRIDE * HEAD_DIM)
    w1r = w1.reshape(2, G, N_CMP_PARTS, CMP_STRIDE * HEAD_DIM, HEAD_DIM)

    def page_spec(p):
        return pl.BlockSpec((None, PAGE_SIZE * slots, HEAD_DIM),
                            lambda bb, s, pt_ref: (layer * n_pool + pt_ref[(bb * nb + s) * n_pages + p], 0, 0))

    return pl.pallas_call(
        functools.partial(_cmp_decode_kernel, G=G, nb=nb, n_pages=n_pages),
        out_shape=jax.ShapeDtypeStruct((2, G, B * n_pages * cpp, HEAD_DIM), F32),
        grid_spec=pltpu.PrefetchScalarGridSpec(
            num_scalar_prefetch=1,
            grid=(B // nb, nb),
            in_specs=[page_spec(p) for p in range(n_pages)] + [
                pl.BlockSpec(pe2.shape, lambda bb, s, pt_ref: (0, 0, 0)),
                pl.BlockSpec(w1r.shape, lambda bb, s, pt_ref: (0, 0, 0, 0, 0)),
                pl.BlockSpec(w2.shape, lambda bb, s, pt_ref: (0, 0, 0, 0))],
            out_specs=pl.BlockSpec((2, G, rows, HEAD_DIM), lambda bb, s, pt_ref: (0, 0, bb, 0)),
            scratch_shapes=[pltpu.VMEM((2 * G, rows, CMP_STRIDE * HEAD_DIM), F32)]),
        compiler_params=_cparams(2),
        name="cmp_decode",
    )(pt, *([pool2] * n_pages), pe2, w1r, w2)


def _nsa_cmp_decode_kernel(q_ref, kcvc_ref, oc_ref, sel_ref, *, G, qpos, nc, ns, nseq):
    scale = HEAD_DIM ** -0.5
    group = NSA_GROUP
    ncp = kcvc_ref.shape[2] // nseq
    nsp = sel_ref.shape[2]
    pooling = _cmp_to_sel_matrix(ncp, nsp, nc)
    imp_sel = []
    for i in range(nseq):
        q = q_ref[i]
        rows = slice(i * ncp, (i + 1) * ncp)
        kcat = jnp.concatenate([kcvc_ref[0, g, rows, :] for g in range(G)], axis=1).astype(BF16)
        vcat = jnp.concatenate([kcvc_ref[1, g, rows, :] for g in range(G)], axis=1).astype(BF16)
        s = _dot_nt(_block_diag(q, G, group).astype(BF16), kcat) * scale
        cidx = lax.broadcasted_iota(jnp.int32, s.shape, 1)
        valid = (cidx * CMP_STRIDE + (CMP_BLOCK - 1) <= qpos) & (cidx < nc)
        s = jnp.where(valid, s, NEG)
        e = jnp.where(valid, jnp.exp(s - jnp.max(s, axis=1, keepdims=True)), 0.0)
        p = e / jnp.maximum(jnp.sum(e, axis=1, keepdims=True), 1e-30)
        oc_ref[i] = _diag_part(_dot(p.astype(BF16), vcat), G, group)
        imp = _expand_rows(jnp.concatenate(
            [jnp.sum(p[g * group:(g + 1) * group], axis=0, keepdims=True) for g in range(G)], axis=0), group, ROWS_PAD)
        imp_sel.append(_dot_exact01(imp, pooling))
    imp_all = jnp.concatenate(imp_sel, axis=0) if nseq > 1 else imp_sel[0]
    sel = _select_blocks(imp_all, jnp.full(imp_all.shape, qpos, jnp.int32), ns, nsp).astype(BF16)
    for i in range(nseq):
        sel_ref[i] = sel[i * ROWS_PAD:(i + 1) * ROWS_PAD]


def _online_tile(s, valid, vcat, m_old, l_old, acc_old, hk):
    s = jnp.where(valid, s, NEG)
    m_new = jnp.maximum(m_old, jnp.max(s, axis=1, keepdims=True))
    e = jnp.where(valid, jnp.exp(s - m_new[:, :1]), 0.0)
    alpha = jnp.exp(m_old - m_new)
    l_new = alpha * l_old + jnp.sum(e, axis=1, keepdims=True)
    acc_new = _rep(alpha, hk) * acc_old + _dot(e.astype(BF16), vcat)
    return m_new, l_new, acc_new


def _nsa_sel_decode_kernel(pt_ref, q_ref, kvn_ref, sel_ref, *refs, G):
    page_refs, o_ref = refs[:-1], refs[-1]
    scale = HEAD_DIM ** -0.5
    group = NSA_GROUP
    slots = 2 * G
    n_keys = len(page_refs) * PAGE_SIZE
    q = q_ref[...]
    k_new = _expand_rows(kvn_ref[slots:slots + G, :], group, ROWS_PAD)
    v_new = _expand_rows(kvn_ref[slots + G:2 * slots, :], group, ROWS_PAD)
    m0 = _lane_rep(jnp.sum(q * k_new, axis=1, keepdims=True) * scale)
    kcat = jnp.concatenate([_gather_cat(r, 0, G, PAGE_SIZE, slots) for r in page_refs], axis=0)
    vcat = jnp.concatenate([_gather_cat(r, G, G, PAGE_SIZE, slots) for r in page_refs], axis=0)
    s = _dot_nt(_block_diag(q, G, group).astype(BF16), kcat) * scale
    nsp = sel_ref.shape[1]
    jidx = lax.broadcasted_iota(jnp.int32, (nsp, n_keys), 0)
    kblk = lax.broadcasted_iota(jnp.int32, (nsp, n_keys), 1) // SEL_BLOCK
    valid = _dot(sel_ref[...], jnp.where(jidx == kblk, 1.0, 0.0).astype(BF16)) > 0.5
    _, l, acc = _online_tile(s, valid, vcat, m0, jnp.ones_like(m0), _block_diag(v_new, G, group), G)
    o_ref[...] = _diag_part(acc, G, group) / jnp.maximum(l, 1e-30)


def _nsa_win_decode_kernel(q_ref, kvn_ref, win_ref, oc_ref, os_ref, gate_ref, o_ref, *, G, wb, past_len):
    scale = HEAD_DIM ** -0.5
    group = NSA_GROUP
    slots = 2 * G
    q = q_ref[...]
    k_new = _expand_rows(kvn_ref[2 * slots:2 * slots + G, :], group, ROWS_PAD)
    v_new = _expand_rows(kvn_ref[2 * slots + G:3 * slots, :], group, ROWS_PAD)
    m0 = _lane_rep(jnp.sum(q * k_new, axis=1, keepdims=True) * scale)
    kcat = _gather_cat(win_ref, 0, G, wb, slots)
    vcat = _gather_cat(win_ref, G, G, wb, slots)
    s = _dot_nt(_block_diag(q, G, group).astype(BF16), kcat) * scale
    wpos = past_len - wb + lax.broadcasted_iota(jnp.int32, s.shape, 1)
    valid = (wpos <= past_len) & (wpos > past_len - WINDOW) & (wpos >= 0)
    _, l, acc = _online_tile(s, valid, vcat, m0, jnp.ones_like(m0), _block_diag(v_new, G, group), G)
    o_w = _diag_part(acc, G, group) / jnp.maximum(l, 1e-30)
    gates = gate_ref[...]
    o_ref[...] = gates[:, 0:1] * oc_ref[...] + gates[:, 1:2] * os_ref[...] + gates[:, 2:3] * o_w


def _nsa_decode(q_n, kv_new, gates, cmp_pool, sel_pool, win_buf, page_table, layer, pe, w1, w2):
    B, n_pages = page_table.shape
    G = cmp_pool.shape[4]
    heads = G * NSA_GROUP
    slots = 2 * G
    past_len = n_pages * PAGE_SIZE
    assert past_len % CMP_STRIDE == 0
    q3 = _pad_rows(q_n.reshape(B, heads, HEAD_DIM))
    kvn3 = _pad_rows(kv_new.reshape(B, 3 * slots, HEAD_DIM))
    g3 = _pad_rows(jnp.pad(gates.reshape(B, heads, 3), ((0, 0), (0, 0), (0, LANES - 3))))
    cpp = PAGE_SIZE // CMP_STRIDE
    ncp = n_pages * cpp
    l_pad = -(-(past_len + 1) // SEL_BLOCK) * SEL_BLOCK
    ns = l_pad // SEL_BLOCK
    nsp = -(-ns // LANES) * LANES
    row_spec = pl.BlockSpec((None, ROWS_PAD, HEAD_DIM), lambda b: (b, 0, 0))
    row_shape = jax.ShapeDtypeStruct((B, ROWS_PAD, HEAD_DIM), F32)

    kcvc = _cmp_decode(cmp_pool, page_table, layer, pe, w1, w2)
    nseq = math.gcd(B, 8)
    seq_spec = pl.BlockSpec((nseq, ROWS_PAD, HEAD_DIM), lambda b: (b, 0, 0))
    o_c, sel = pl.pallas_call(
        functools.partial(_nsa_cmp_decode_kernel, G=G, qpos=past_len, nc=ncp, ns=ns, nseq=nseq),
        out_shape=(row_shape, jax.ShapeDtypeStruct((B, ROWS_PAD, nsp), BF16)),
        grid=(B // nseq,),
        in_specs=[seq_spec, pl.BlockSpec((2, G, nseq * ncp, HEAD_DIM), lambda b: (0, 0, b, 0))],
        out_specs=(seq_spec, pl.BlockSpec((nseq, ROWS_PAD, nsp), lambda b: (b, 0, 0))),
        compiler_params=_cparams(1),
        name="nsa_cmp_decode",
    )(q3, kcvc)
    n_pool = sel_pool.shape[1]
    sel_pool2 = sel_pool.reshape(sel_pool.shape[0] * n_pool, PAGE_SIZE * slots, HEAD_DIM)
    o_s = _paged_decode(functools.partial(_nsa_sel_decode_kernel, G=G), [(sel_pool2, n_pool)], page_table, layer,
                        [q3, kvn3, sel], [], [], pps=n_pages, reverse=False, name="nsa_sel_decode")
    wb = win_buf.shape[2]
    win2 = win_buf.reshape(win_buf.shape[0] * B, wb * slots, HEAD_DIM)
    o_n = pl.pallas_call(
        functools.partial(_nsa_win_decode_kernel, G=G, wb=wb, past_len=past_len),
        out_shape=row_shape,
        grid=(B,),
        in_specs=[row_spec, row_spec,
                  pl.BlockSpec((None, wb * slots, HEAD_DIM), lambda b: (layer * B + b, 0, 0)),
                  row_spec, row_spec, row_spec],
        out_specs=row_spec,
        compiler_params=_cparams(1),
        name="nsa_win_decode",
    )(q3, kvn3, win2, o_c, o_s, g3)
    return o_n[:, :heads].reshape(B, heads * HEAD_DIM)


def _mem_decode_kernel(q_ref, kv_ref, o_ref, *, n_mem):
    scale = HEAD_DIM ** -0.5
    slots = 2 * MEM_H
    kcat = _gather_cat(kv_ref, 0, MEM_H, n_mem, slots)
    vcat = _gather_cat(kv_ref, MEM_H, MEM_H, n_mem, slots)
    s = _dot_nt(_block_diag(q_ref[...], MEM_H, 1).astype(BF16), kcat) * scale
    e = jnp.exp(s - jnp.max(s, axis=1, keepdims=True))
    p = e / jnp.sum(e, axis=1, keepdims=True)
    o_ref[...] = _diag_part(_dot(p.astype(BF16), vcat), MEM_H, 1)


def _mem_decode(qz, cache_mem, layer):
    B = qz.shape[0]
    n_mem = cache_mem.shape[2]
    slots = 2 * MEM_H
    E = MEM_H * HEAD_DIM
    q3 = _pad_rows(qz[:, :E].reshape(B, MEM_H, HEAD_DIM))
    kv2 = cache_mem.reshape(cache_mem.shape[0] * B, n_mem * slots, HEAD_DIM)
    row_spec = pl.BlockSpec((None, ROWS_PAD, HEAD_DIM), lambda b: (b, 0, 0))
    o = pl.pallas_call(
        functools.partial(_mem_decode_kernel, n_mem=n_mem),
        out_shape=jax.ShapeDtypeStruct((B, ROWS_PAD, HEAD_DIM), F32),
        grid=(B,),
        in_specs=[row_spec, pl.BlockSpec((None, n_mem * slots, HEAD_DIM), lambda b: (layer * B + b, 0, 0))],
        out_specs=row_spec,
        compiler_params=_cparams(1),
        name="mem_decode",
    )(q3, kv2)
    return o[:, :MEM_H].reshape(B, E)


TM_PROJ = 1024
TN_PROJ = 512
TM_OUT = 512


def _even_weights(w_in, w_out, cmp_w1, cmp_w2):
    G = cmp_w1.shape[1]
    hd = HEAD_DIM
    nq = G * NSA_GROUP * hd
    sizes = (nq, 6 * G * hd, 3 * G * NSA_GROUP, nq)
    c0, c1, c2, c3 = (sum(sizes[:k + 1]) for k in range(4))
    n_rest = w_in.shape[1] - c3
    sb_w = n_rest // 4
    gate = w_in[:, c1:c2].reshape(-1, G, NSA_GROUP * 3)
    gate_grouped = jnp.pad(gate, ((0, 0), (0, 0), (0, LANES - NSA_GROUP * 3))).reshape(-1, G * LANES)
    gate_flat = jnp.pad(w_in[:, c1:c2], ((0, 0), (0, LANES - (c2 - c1))))
    return dict(
        q_n=w_in[:, :c0].astype(BF16),
        kv_n=w_in[:, c0:c1].astype(BF16),
        gate_grouped=gate_grouped.astype(BF16),
        gate_flat=gate_flat.astype(BF16),
        z=jnp.concatenate([w_in[:, c2:c3], w_in[:, c3 + 3 * sb_w:]], axis=1).astype(BF16),
        sb_q=w_in[:, c3:c3 + sb_w].astype(BF16),
        sb_kv=w_in[:, c3 + sb_w:c3 + 3 * sb_w].astype(BF16),
        out=w_out.astype(BF16),
        w1=cmp_w1.astype(BF16),
        w2=cmp_w2.astype(BF16),
        G=G, sb_h=sb_w // hd,
    )


def _odd_weights(w_in, b_f, w_out):
    n_h = b_f.shape[0]
    c1 = 3 * n_h * HEAD_DIM
    return dict(
        q=w_in[:, :c1 // 3].astype(BF16),
        kv=w_in[:, c1 // 3:c1].astype(BF16),
        f=jnp.pad(w_in[:, c1:c1 + n_h], ((0, 0), (0, LANES - n_h))).astype(BF16),
        b_f=jnp.pad(b_f, (0, LANES - n_h)),
        z=w_in[:, c1 + n_h:].astype(BF16),
        out=w_out.astype(BF16),
        n_h=n_h,
    )


def _proj(x, g, w, rope=None, rope_heads=(), **kw):
    tn = TN_PROJ if w.shape[1] % TN_PROJ == 0 else min(w.shape[1], 256)
    return _norm_proj(x, g, w, tm=TM_PROJ, tn=tn, rope=rope, rope_heads=rope_heads, **kw)


def _even_project(x, g, W, rope):
    q_n = _proj(x, g, W["q_n"], rope, (1, 1, 1, 1))
    kv_n = _proj(x, g, W["kv_n"], rope, (1, 1, 0, 0))
    z = _proj(x, g, W["z"])
    sb_q = _proj(x, g, W["sb_q"])
    sb_kv = _proj(x, g, W["sb_kv"])
    return q_n, kv_n, z, sb_q, sb_kv


def _mem_block(xp, xs, i, mem_prompt, cache_mem, mx_norm, mx_mem_norm, mx_w_qz, mx_w_kv, mx_w_o):
    w_qz = mx_w_qz[i].astype(BF16)
    w_o = mx_w_o[i].astype(BF16)
    kv_mem = _proj(mem_prompt, mx_mem_norm[i], mx_w_kv[i].astype(BF16))
    qz = _proj(xp, mx_norm[i], w_qz)
    xp = _gated_out([_mem_attention(qz, kv_mem)], qz, 1, w_o, xp, tm=TM_OUT, tn=TN_PROJ)
    qz = _proj(xs, mx_norm[i], w_qz)
    xs = _gated_out([_mem_decode(qz, cache_mem, i)], qz, 1, w_o, xs, tm=TM_OUT, tn=TN_PROJ)
    return xp, xs, kv_mem


def kernel(x_prompt, x_sample, cache_nsa_cmp, cache_nsa_sel, cache_nsa_win, cache_sb, cache_fox_kv, cache_fox_logf, cache_mem, page_table, mem_prompt, ev_norm, ev_w_in, ev_cmp_pe, ev_cmp_w1, ev_cmp_w2, ev_w_out, od_norm, od_w_in, od_b_f, od_w_out, mx_norm, mx_mem_norm, mx_w_qz, mx_w_kv, mx_w_o, final_norm):
    bp, T, D = x_prompt.shape
    B, t_s, _ = x_sample.shape
    assert bp == 1 and t_s == 1
    depth = mx_norm.shape[0]
    n_pages = page_table.shape[1]
    past_len = n_pages * PAGE_SIZE
    rope_p = _rope_tables(jnp.arange(T, dtype=jnp.int32))
    rope_s = _rope_tables(jnp.full((B,), past_len, jnp.int32))
    xp = x_prompt.reshape(T, D)
    xs = x_sample.reshape(B, D)
    mem = mem_prompt.reshape(mem_prompt.shape[1], D)
    outs = {k: [] for k in ("cmp_p", "cmp_s", "sel_p", "sel_s", "win_p", "win_s", "sb_p", "sb_s",
                            "fkv_p", "fkv_s", "flf_p", "flf_s", "mem_p")}
    for i in range(depth):
        if i % 2 == 0:
            e = i // 2
            W = _even_weights(ev_w_in[e], ev_w_out[e], ev_cmp_w1[e], ev_cmp_w2[e])
            G, sb_h = W["G"], W["sb_h"]
            bw = 2 * G * HEAD_DIM
            w1_chunks = W["w1"].reshape(2, G, CMP_BLOCK, HEAD_DIM, HEAD_DIM)
            q_n, kv_n, z, sb_q, sb_kv = _even_project(xp, ev_norm[e], W, rope_p)
            gates = _proj(xp, ev_norm[e], W["gate_grouped"], act="sigmoid")
            kc_vc = _nsa_compress(kv_n, ev_cmp_pe[e], w1_chunks, W["w2"])
            o_n = _nsa_prompt(q_n, kv_n, gates, kc_vc)
            o_s = _sb_attention(sb_q, sb_kv, sb_h)
            xp = _gated_out([o_n, o_s], z, 0, W["out"], xp, tm=TM_OUT, tn=TN_PROJ)
            wbp = min(WINDOW, T)
            outs["cmp_p"].append(kv_n[:, :bw].reshape(1, T, 2, G, HEAD_DIM))
            outs["sel_p"].append(kv_n[:, bw:2 * bw].reshape(1, T, 2, G, HEAD_DIM))
            outs["win_p"].append(kv_n[T - wbp:, 2 * bw:].reshape(1, wbp, 2, G, HEAD_DIM))
            outs["sb_p"].append(sb_kv.reshape(1, T, 2, sb_h, HEAD_DIM))
            q_n, kv_n, z, sb_q, sb_kv = _even_project(xs, ev_norm[e], W, rope_s)
            gates = _proj(xs, ev_norm[e], W["gate_flat"], act="sigmoid")[:, :3 * G * NSA_GROUP]
            o_n = _nsa_decode(q_n, kv_n, gates, cache_nsa_cmp, cache_nsa_sel, cache_nsa_win, page_table, e,
                              ev_cmp_pe[e], W["w1"], W["w2"])
            o_s = _sb_decode(sb_q, cache_sb, page_table, e)
            xs = _gated_out([o_n, o_s], z, 0, W["out"], xs, tm=TM_OUT, tn=TN_PROJ)
            kv5 = kv_n.reshape(B, 1, 3, 2, G, HEAD_DIM)
            outs["cmp_s"].append(kv5[:, :, 0])
            outs["sel_s"].append(kv5[:, :, 1])
            outs["win_s"].append(jnp.concatenate([cache_nsa_win[e][:, t_s:], kv5[:, :, 2]], axis=1))
            outs["sb_s"].append(sb_kv.reshape(B, 1, 2, sb_h, HEAD_DIM))
        else:
            o = i // 2
            W = _odd_weights(od_w_in[o], od_b_f[o], od_w_out[o])
            n_h = W["n_h"]
            q = _proj(xp, od_norm[o], W["q"])
            kv = _proj(xp, od_norm[o], W["kv"])
            z = _proj(xp, od_norm[o], W["z"])
            logf = _proj(xp, od_norm[o], W["f"], bias=W["b_f"], act="logsigmoid")
            f_cum = _cumsum_rows(logf)[:, :n_h]
            xp = _gated_out([_fox_attention(q, kv, f_cum, n_h)], z, 0, W["out"], xp, tm=TM_OUT, tn=TN_PROJ)
            outs["fkv_p"].append(kv.reshape(1, T, 2, n_h, HEAD_DIM))
            outs["flf_p"].append(logf[:, :n_h].reshape(1, T, n_h))
            q = _proj(xs, od_norm[o], W["q"])
            kv = _proj(xs, od_norm[o], W["kv"])
            z = _proj(xs, od_norm[o], W["z"])
            logf = _proj(xs, od_norm[o], W["f"], bias=W["b_f"], act="logsigmoid")[:, :n_h]
            att = _fox_decode(q, kv, logf, cache_fox_kv, cache_fox_logf, page_table, o)
            xs = _gated_out([att], z, 0, W["out"], xs, tm=TM_OUT, tn=TN_PROJ)
            outs["fkv_s"].append(kv.reshape(B, 1, 2, n_h, HEAD_DIM))
            outs["flf_s"].append(logf.reshape(B, 1, n_h))
        xp, xs, kv_mem = _mem_block(xp, xs, i, mem, cache_mem, mx_norm, mx_mem_norm, mx_w_qz, mx_w_kv, mx_w_o)
        outs["mem_p"].append(kv_mem.reshape(1, mem.shape[0], 2, MEM_H, HEAD_DIM))
    y_prompt = _rmsnorm(xp, final_norm, tm=TM_OUT).reshape(1, T, D)
    y_sample = _rmsnorm(xs, final_norm, tm=TM_OUT).reshape(B, 1, D)
    return (y_prompt, y_sample) + tuple(
        jnp.stack(outs[k]) for k in ("cmp_p", "cmp_s", "sel_p", "sel_s", "win_p", "win_s", "sb_p", "sb_s",
                                     "fkv_p", "fkv_s", "flf_p", "flf_s", "mem_p"))
```

```python
import functools
import math

import jax
import jax.numpy as jnp
from jax import lax
from jax.experimental import pallas as pl
from jax.experimental.pallas import tpu as pltpu

F32 = jnp.float32
BF16 = jnp.bfloat16

HEAD_DIM = 128
PAGE_SIZE = 128
NSA_GROUP = 4
CMP_BLOCK = 32
CMP_STRIDE = 16
SEL_BLOCK = 64
SEL_TOPK = 16
WINDOW = 512
MEM_H = 4
ROPE_THETA = 10000.0
EPS = 1e-6
NEG = -1e30
SB_DEAD = -150.0

LANES = 128
VMEM_LIMIT = 52 * 1024 * 1024


def _cparams(n_axes):
    return pltpu.CompilerParams(dimension_semantics=("arbitrary",) * n_axes,
                                vmem_limit_bytes=VMEM_LIMIT)


def _sigmoid(x):
    return 1.0 / (1.0 + jnp.exp(-x))


def _softplus(x):
    return jnp.maximum(x, 0.0) + jnp.log(1.0 + jnp.exp(-jnp.abs(x)))


def _dot(a, b):
    return jnp.dot(a, b, preferred_element_type=F32)


def _dot_nt(a, b):
    return lax.dot_general(a, b, (((1,), (1,)), ((), ())), preferred_element_type=F32)


def _split3(x):
    h1 = x.astype(BF16)
    r1 = x - h1.astype(F32)
    h2 = r1.astype(BF16)
    h3 = (r1 - h2.astype(F32)).astype(BF16)
    return h1, h2, h3


def _dot_exact01(x, m01, parts=3):
    hs = _split3(x)[:parts]
    m = x.shape[0]
    if m % 16 == 0 and m <= 64:
        r = _dot(jnp.concatenate(hs, axis=0), m01)
        acc = r[0:m]
        for a in range(1, parts):
            acc = acc + r[a * m:(a + 1) * m]
        return acc
    acc = _dot(hs[0], m01)
    for h in hs[1:]:
        acc = acc + _dot(h, m01)
    return acc


def _rope_tables(pos):
    half = HEAD_DIM // 2
    inv = ROPE_THETA ** (-jnp.arange(half, dtype=F32) / half)
    ang = pos.astype(F32)[:, None] * inv[None, :]
    cos = jnp.cos(ang)
    sin = jnp.sin(ang)
    return jnp.concatenate([cos, cos], axis=1), jnp.concatenate([-sin, sin], axis=1)


def _proj_kernel(*refs, rope_heads, has_bias, act):
    x_ref, g_ref, w_ref = refs[:3]
    k = 3
    if any(rope_heads):
        cos_ref, sin_ref = refs[k], refs[k + 1]
        k += 2
    if has_bias:
        b_ref = refs[k]
        k += 1
    o_ref, xn_ref = refs[k], refs[k + 1]

    @pl.when(pl.program_id(1) == 0)
    def _():
        x = x_ref[...]
        r = lax.rsqrt(jnp.mean(x * x, axis=-1, keepdims=True) + EPS)
        xn_ref[...] = ((x * r) * g_ref[...]).astype(BF16)

    acc = _dot(xn_ref[...], w_ref[...])
    if has_bias:
        acc = acc + b_ref[...]
    if act == "sigmoid":
        acc = _sigmoid(acc)
    elif act == "logsigmoid":
        acc = -_softplus(-acc)
    if any(rope_heads):
        cos2 = cos_ref[...]
        sin2 = sin_ref[...]
        for hh, flag in enumerate(rope_heads):
            sl = slice(hh * HEAD_DIM, (hh + 1) * HEAD_DIM)
            xh = acc[:, sl]
            if flag:
                xh = xh * cos2 + pltpu.roll(xh, HEAD_DIM // 2, axis=1) * sin2
            o_ref[:, sl] = xh
    else:
        o_ref[...] = acc


def _norm_proj(x, g, w, *, tm, tn, rope=None, rope_heads=(), bias=None, act=None):
    M, D = x.shape
    N = w.shape[1]
    tm = min(tm, M)
    tn = min(tn, N)
    assert M % tm == 0 and N % tn == 0
    in_specs = [pl.BlockSpec((tm, D), lambda i, j: (i, 0)),
                pl.BlockSpec((1, D), lambda i, j: (0, 0)),
                pl.BlockSpec((D, tn), lambda i, j: (0, j))]
    args = [x, g.reshape(1, D), w]
    if any(rope_heads):
        assert len(rope_heads) * HEAD_DIM == tn
        in_specs += [pl.BlockSpec((tm, HEAD_DIM), lambda i, j: (i, 0))] * 2
        args += list(rope)
    if bias is not None:
        in_specs.append(pl.BlockSpec((1, tn), lambda i, j: (0, j)))
        args.append(bias.reshape(1, N))
    return pl.pallas_call(
        functools.partial(_proj_kernel, rope_heads=tuple(rope_heads), has_bias=bias is not None, act=act),
        out_shape=jax.ShapeDtypeStruct((M, N), F32),
        grid=(M // tm, N // tn),
        in_specs=in_specs,
        out_specs=pl.BlockSpec((tm, tn), lambda i, j: (i, j)),
        scratch_shapes=[pltpu.VMEM((tm, D), BF16)],
        compiler_params=_cparams(2),
        name="norm_proj",
    )(*args)


def _out_kernel(*refs, widths):
    n = len(widths)
    o_refs = refs[:n]
    z_ref, w_ref, res_ref, y_ref, mix_ref = refs[n:n + 5]

    @pl.when(pl.program_id(1) == 0)
    def _():
        off = 0
        for o_ref, wd in zip(o_refs, widths):
            z = z_ref[:, off:off + wd]
            mix_ref[:, off:off + wd] = (o_ref[...] * (z * _sigmoid(z))).astype(BF16)
            off += wd

    y_ref[...] = res_ref[...] + _dot(mix_ref[...], w_ref[...])


def _gated_out(o_parts, z, z_block, w, res, *, tm, tn):
    M, N = res.shape
    widths = tuple(o.shape[1] for o in o_parts)
    E = sum(widths)
    assert w.shape == (E, N)
    tm = min(tm, M)
    tn = min(tn, N)
    assert M % tm == 0 and N % tn == 0
    in_specs = [pl.BlockSpec((tm, wd), lambda i, j: (i, 0)) for wd in widths]
    in_specs += [pl.BlockSpec((tm, E), lambda i, j: (i, z_block)),
                 pl.BlockSpec((E, tn), lambda i, j: (0, j)),
                 pl.BlockSpec((tm, tn), lambda i, j: (i, j))]
    return pl.pallas_call(
        functools.partial(_out_kernel, widths=widths),
        out_shape=jax.ShapeDtypeStruct((M, N), F32),
        grid=(M // tm, N // tn),
        in_specs=in_specs,
        out_specs=pl.BlockSpec((tm, tn), lambda i, j: (i, j)),
        scratch_shapes=[pltpu.VMEM((tm, E), BF16)],
        compiler_params=_cparams(2),
        name="gated_out",
    )(*o_parts, z, w, res)


def _norm_kernel(x_ref, g_ref, o_ref):
    x = x_ref[...]
    r = lax.rsqrt(jnp.mean(x * x, axis=-1, keepdims=True) + EPS)
    o_ref[...] = (x * r) * g_ref[...]


def _rmsnorm(x, g, *, tm):
    M, D = x.shape
    tm = min(tm, M)
    return pl.pallas_call(
        _norm_kernel,
        out_shape=jax.ShapeDtypeStruct((M, D), F32),
        grid=(M // tm,),
        in_specs=[pl.BlockSpec((tm, D), lambda i: (i, 0)), pl.BlockSpec((1, D), lambda i: (0, 0))],
        out_specs=pl.BlockSpec((tm, D), lambda i: (i, 0)),
        compiler_params=_cparams(1),
        name="final_norm",
    )(x, g.reshape(1, D))


def _cumsum_kernel(x_ref, o_ref, carry_ref, *, tt):
    @pl.when(pl.program_id(0) == 0)
    def _():
        carry_ref[...] = jnp.zeros_like(carry_ref)

    r = lax.broadcasted_iota(jnp.int32, (tt, tt), 0)
    c = lax.broadcasted_iota(jnp.int32, (tt, tt), 1)
    tri = jnp.where(c <= r, 1.0, 0.0).astype(BF16)
    x = x_ref[...]
    h1, h2, h3 = _split3(x)
    cs = _dot(tri, h1) + _dot(tri, h2) + _dot(tri, h3) + carry_ref[...]
    o_ref[...] = cs
    carry_ref[...] = cs[tt - 1:tt, :]


def _cumsum_rows(x, *, tt=256):
    T, N = x.shape
    tt = min(tt, T)
    assert T % tt == 0
    return pl.pallas_call(
        functools.partial(_cumsum_kernel, tt=tt),
        out_shape=jax.ShapeDtypeStruct((T, N), F32),
        grid=(T // tt,),
        in_specs=[pl.BlockSpec((tt, N), lambda i: (i, 0))],
        out_specs=pl.BlockSpec((tt, N), lambda i: (i, 0)),
        scratch_shapes=[pltpu.VMEM((1, N), F32)],
        compiler_params=_cparams(1),
        name="cumsum_rows",
    )(x)


def _fox_kernel(q_ref, k_ref, v_ref, fq_ref, fk_ref, o_ref, *, tq, tk):
    i = pl.program_id(1)
    scale = HEAD_DIM ** -0.5
    q = q_ref[...].astype(BF16)
    fq = jnp.concatenate([fq_ref[...]] * (tk // LANES), axis=1)

    def tile(kt, carry, masked):
        m, l, acc = carry
        start = pl.multiple_of(kt * tk, tk)
        k = k_ref[pl.ds(start, tk), :].astype(BF16)
        v = v_ref[pl.ds(start, tk), :].astype(BF16)
        s = _dot_nt(q, k) * scale
        s = s + fq - fk_ref[:, pl.ds(start, tk)]
        if masked:
            qpos = i * tq + lax.broadcasted_iota(jnp.int32, (tq, tk), 0)
            kpos = start + lax.broadcasted_iota(jnp.int32, (tq, tk), 1)
            valid = kpos <= qpos
            s = jnp.where(valid, s, NEG)
        m_new = jnp.maximum(m, jnp.max(s, axis=1, keepdims=True))
        e = jnp.exp(s - m_new)
        if masked:
            e = jnp.where(valid, e, 0.0)
        alpha = jnp.exp(m - m_new)
        l = alpha * l + jnp.sum(e, axis=1, keepdims=True)
        acc = alpha * acc + _dot(e.astype(BF16), v)
        return m_new, l, acc

    init = (jnp.full((tq, 1), NEG, F32), jnp.zeros((tq, 1), F32), jnp.zeros((tq, HEAD_DIM), F32))
    n_full = (i * tq) // tk
    carry = lax.fori_loop(0, n_full, lambda kt, c: tile(kt, c, False), init)
    for d in range(tq // tk):
        carry = tile(n_full + d, carry, True)
    m, l, acc = carry
    o_ref[...] = acc / jnp.maximum(l, 1e-30)


def _fox_attention(q, kv, f_cum, n_heads, *, tq=512, tk=512):
    T = q.shape[0]
    tq = min(tq, T)
    tk = min(tk, tq)
    assert T % tq == 0 and tq % tk == 0
    f_t = f_cum.T
    f_row = f_t.reshape(n_heads, 1, T)
    f_rep = jnp.broadcast_to(f_t[:, :, None], (n_heads, T, LANES))
    return pl.pallas_call(
        functools.partial(_fox_kernel, tq=tq, tk=tk),
        out_shape=jax.ShapeDtypeStruct((T, n_heads * HEAD_DIM), F32),
        grid=(n_heads, T // tq),
        in_specs=[pl.BlockSpec((tq, HEAD_DIM), lambda h, i: (i, h)),
                  pl.BlockSpec((T, HEAD_DIM), lambda h, i: (0, h)),
                  pl.BlockSpec((T, HEAD_DIM), lambda h, i: (0, n_heads + h)),
                  pl.BlockSpec((None, tq, LANES), lambda h, i: (h, i, 0)),
                  pl.BlockSpec((None, 1, T), lambda h, i: (h, 0, 0))],
        out_specs=pl.BlockSpec((tq, HEAD_DIM), lambda h, i: (i, h)),
        compiler_params=_cparams(2),
        name="fox_attention",
    )(q, kv, kv, f_rep, f_row)


def _sb_kernel(q_ref, k_ref, v_ref, o_ref, *, tq):
    i = pl.program_id(1)
    tk = tq
    scale = HEAD_DIM ** -0.5
    q = q_ref[...].astype(BF16)
    r = lax.broadcasted_iota(jnp.int32, (LANES, LANES), 0)
    c = lax.broadcasted_iota(jnp.int32, (LANES, LANES), 1)
    suffix = jnp.where(r > c, 1.0, 0.0).astype(BF16)

    def tile(kt, c_in, acc, masked):
        start = pl.multiple_of(kt * tk, tk)
        k = k_ref[pl.ds(start, tk), :].astype(BF16)
        v = v_ref[pl.ds(start, tk), :].astype(BF16)
        z = _dot_nt(q, k) * scale
        sp = _softplus(z)
        log_keep = -sp
        if masked:
            valid = lax.broadcasted_iota(jnp.int32, (tq, tk), 1) < lax.broadcasted_iota(jnp.int32, (tq, tk), 0)
            log_keep = jnp.where(valid, log_keep, 0.0)
        carry = c_in
        after = [None] * (tk // LANES)
        for t in reversed(range(tk // LANES)):
            chunk = log_keep[:, t * LANES:(t + 1) * LANES]
            after[t] = _dot_exact01(chunk, suffix, parts=2) + carry
            carry = carry + jnp.sum(chunk, axis=1, keepdims=True)
        a = jnp.exp((z - sp) + (jnp.concatenate(after, axis=1) if len(after) > 1 else after[0]))
        if masked:
            a = jnp.where(valid, a, 0.0)
        acc = acc + _dot(a.astype(BF16), v)
        return carry, acc

    c0, acc0 = tile(i, jnp.zeros((tq, LANES), F32), jnp.zeros((tq, HEAD_DIM), F32), True)

    def cond(st):
        kt, alive, _, _ = st
        return jnp.logical_and(kt >= 0, alive)

    def body(st):
        kt, _, c_in, acc = st
        c_out, acc = tile(kt, c_in, acc, False)
        return kt - 1, jnp.max(c_out) > SB_DEAD, c_out, acc

    _, _, _, acc = lax.while_loop(cond, body, (i - 1, jnp.max(c0) > SB_DEAD, c0, acc0))
    o_ref[...] = acc


def _sb_attention(q, kv, n_heads, *, tq=256):
    T = q.shape[0]
    tq = min(tq, T)
    assert T % tq == 0
    return pl.pallas_call(
        functools.partial(_sb_kernel, tq=tq),
        out_shape=jax.ShapeDtypeStruct((T, n_heads * HEAD_DIM), F32),
        grid=(n_heads, T // tq),
        in_specs=[pl.BlockSpec((tq, HEAD_DIM), lambda h, i: (i, h)),
                  pl.BlockSpec((T, HEAD_DIM), lambda h, i: (0, h)),
                  pl.BlockSpec((T, HEAD_DIM), lambda h, i: (0, n_heads + h))],
        out_specs=pl.BlockSpec((tq, HEAD_DIM), lambda h, i: (i, h)),
        compiler_params=_cparams(2),
        name="sb_attention",
    )(q, kv, kv)


N_CMP_PARTS = CMP_BLOCK // CMP_STRIDE
SEL_PER_CMP = SEL_BLOCK // CMP_STRIDE
SEL_TILE = 512


def _compress_mlp(chunk_sums, w2_ref, o_ref):
    n = chunk_sums[0].shape[0]
    h = chunk_sums[0]
    for p in range(1, N_CMP_PARTS):
        h = h + pltpu.roll(chunk_sums[p], n - p, axis=0)
    h = h * _sigmoid(h)
    o_ref[...] = _dot(h.astype(BF16), w2_ref[...])


def _compress_kernel(rows_ref, pe_ref, w1_ref, w2_ref, o_ref, *, n_chunks):
    sums = []
    for p in range(N_CMP_PARTS):
        acc = jnp.zeros((n_chunks, HEAD_DIM), F32)
        for j in range(CMP_STRIDE):
            jj = p * CMP_STRIDE + j
            x = rows_ref[pl.ds(j, n_chunks, stride=CMP_STRIDE), :]
            acc = acc + _dot((x + pe_ref[jj:jj + 1, :]).astype(BF16), w1_ref[jj])
        sums.append(acc)
    _compress_mlp(sums, w2_ref, o_ref)


def _nsa_compress(kv_n, pe, w1, w2):
    T = kv_n.shape[0]
    G = w1.shape[1]
    n_chunks = T // CMP_STRIDE
    return pl.pallas_call(
        functools.partial(_compress_kernel, n_chunks=n_chunks),
        out_shape=jax.ShapeDtypeStruct((2, G, n_chunks, HEAD_DIM), F32),
        grid=(2, G),
        in_specs=[pl.BlockSpec((T, HEAD_DIM), lambda a, g: (0, a * G + g)),
                  pl.BlockSpec((None, CMP_BLOCK, HEAD_DIM), lambda a, g: (a, 0, 0)),
                  pl.BlockSpec((None, None, CMP_BLOCK, HEAD_DIM, HEAD_DIM), lambda a, g: (a, g, 0, 0, 0)),
                  pl.BlockSpec((None, None, HEAD_DIM, HEAD_DIM), lambda a, g: (a, g, 0, 0))],
        out_specs=pl.BlockSpec((None, None, n_chunks, HEAD_DIM), lambda a, g: (a, g, 0, 0)),
        compiler_params=_cparams(2),
        name="nsa_compress",
    )(kv_n, pe, w1, w2)


def _stack_heads(q):
    r = q.shape[1] // HEAD_DIM
    return jnp.concatenate([q[:, a * HEAD_DIM:(a + 1) * HEAD_DIM] for a in range(r)], axis=0)


def _unstack_heads(o_ref, o, tq):
    for a in range(o.shape[0] // tq):
        o_ref[:, a * HEAD_DIM:(a + 1) * HEAD_DIM] = o[a * tq:(a + 1) * tq, :]


def _select_blocks(imp_sel, qpos, ns, nsp):
    rows = imp_sel.shape[0]
    blk_i = lax.broadcasted_iota(jnp.int32, (rows, nsp), 1)
    blk = blk_i.astype(F32)
    visible = blk_i * SEL_BLOCK <= qpos
    is_cur = blk_i == qpos // SEL_BLOCK
    score = jnp.where(is_cur, 1e9, jnp.where(visible, imp_sel, -1e9))
    score = jnp.where(blk_i < ns, score, -3e38)

    def pick(_, st):
        score, sel = st
        m = jnp.max(score, axis=1, keepdims=True)
        idx = jnp.min(jnp.where(score == m, blk, float(nsp)), axis=1, keepdims=True)
        hit = blk == idx
        sel = jnp.where(hit & (m > -1e8), 1.0, sel)
        return jnp.where(hit, -3e38, score), sel

    _, sel = lax.fori_loop(0, min(SEL_TOPK, ns), pick, (score, jnp.zeros((rows, nsp), F32)))
    return sel


def _cmp_to_sel_matrix(ncp, nsp, nc):
    c = lax.broadcasted_iota(jnp.int32, (ncp, nsp), 0)
    j = lax.broadcasted_iota(jnp.int32, (ncp, nsp), 1)
    lo = SEL_PER_CMP * j - (N_CMP_PARTS - 1)
    hit = (c >= lo) & (c <= lo + SEL_PER_CMP + N_CMP_PARTS - 2) & (c < nc)
    return jnp.where(hit, 1.0, 0.0).astype(BF16)


def _nsa_cmp_kernel(q_ref, kcvc_ref, oc_ref, sel_ref, *, tq, nc, ns):
    i = pl.program_id(0)
    G = kcvc_ref.shape[1]
    ncp = kcvc_ref.shape[2]
    nsp = sel_ref.shape[2]
    gw = q_ref.shape[1] // G
    scale = HEAD_DIM ** -0.5
    pooling = _cmp_to_sel_matrix(ncp, nsp, nc)
    imp_sel = []
    for g in range(G):
        q = _stack_heads(q_ref[:, g * gw:(g + 1) * gw]).astype(BF16)
        rows = q.shape[0]
        s = _dot_nt(q, kcvc_ref[0, g].astype(BF16)) * scale
        qpos = i * tq + (lax.broadcasted_iota(jnp.int32, (rows, ncp), 0) & (tq - 1))
        cidx = lax.broadcasted_iota(jnp.int32, (rows, ncp), 1)
        valid = (cidx * CMP_STRIDE + (CMP_BLOCK - 1) <= qpos) & (cidx < nc)
        s = jnp.where(valid, s, NEG)
        e = jnp.where(valid, jnp.exp(s - jnp.max(s, axis=1, keepdims=True)), 0.0)
        p = e / jnp.maximum(jnp.sum(e, axis=1, keepdims=True), 1e-30)
        _unstack_heads(oc_ref.at[:, g * gw:(g + 1) * gw], _dot(p.astype(BF16), kcvc_ref[1, g].astype(BF16)), tq)
        imp = p[0:tq]
        for a in range(1, rows // tq):
            imp = imp + p[a * tq:(a + 1) * tq]
        imp_sel.append(_dot_exact01(imp, pooling))
    imp_all = jnp.concatenate(imp_sel, axis=0)
    qpos1 = i * tq + (lax.broadcasted_iota(jnp.int32, imp_all.shape, 0) & (tq - 1))
    sel = _select_blocks(imp_all, qpos1, ns, nsp).astype(BF16)
    for g in range(G):
        sel_ref[g] = sel[g * tq:(g + 1) * tq]


M_INIT = -1e20


def _nsa_sel_kernel(q_ref, k_ref, v_ref, sel_ref, o_ref, *, tq, tk):
    i = pl.program_id(1)
    nsp = sel_ref.shape[1]
    scale = HEAD_DIM ** -0.5
    q = _stack_heads(q_ref[...]).astype(BF16)
    heads = q.shape[0] // tq
    sel = sel_ref[...]
    jidx = lax.broadcasted_iota(jnp.int32, (nsp, tk), 0)
    koff = lax.broadcasted_iota(jnp.int32, (nsp, tk), 1) // SEL_BLOCK
    qpos = i * tq + lax.broadcasted_iota(jnp.int32, (tq, tk), 0)
    kidx = lax.broadcasted_iota(jnp.int32, (tq, tk), 1)

    def tile(kt, carry):
        m, l, acc = carry
        start = pl.multiple_of(kt * tk, tk)
        k = k_ref[pl.ds(start, tk), :].astype(BF16)
        v = v_ref[pl.ds(start, tk), :].astype(BF16)
        expand = jnp.where(jidx == kt * (tk // SEL_BLOCK) + koff, 1.0, 0.0).astype(BF16)
        chosen = _dot(sel, expand)
        bias1 = jnp.where((chosen > 0.5) & (start + kidx <= qpos), 0.0, NEG)
        s = _dot_nt(q, k) * scale + jnp.concatenate([bias1] * heads, axis=0)
        m_new = jnp.maximum(m, jnp.max(s, axis=1, keepdims=True))
        e = jnp.exp(s - m_new)
        alpha = jnp.exp(m - m_new)
        l = alpha * l + jnp.sum(e, axis=1, keepdims=True)
        acc = alpha * acc + _dot(e.astype(BF16), v)
        return m_new, l, acc

    rows = q.shape[0]
    init = (jnp.full((rows, 1), M_INIT, F32), jnp.zeros((rows, 1), F32), jnp.zeros((rows, HEAD_DIM), F32))
    _, l, acc = lax.fori_loop(0, ((i + 1) * tq + tk - 1) // tk, tile, init)
    _unstack_heads(o_ref, acc / jnp.maximum(l, 1e-30), tq)


def _nsa_win_kernel(q_ref, k_ref, v_ref, oc_ref, os_ref, gate_ref, o_ref, *, tq, wk):
    i = pl.program_id(1)
    scale = HEAD_DIM ** -0.5
    q = _stack_heads(q_ref[...]).astype(BF16)
    heads = q.shape[0] // tq
    n_keys = k_ref.shape[0]
    start = pl.multiple_of(jnp.clip(i * tq - WINDOW, 0, n_keys - wk), tq)
    k = k_ref[pl.ds(start, wk), :].astype(BF16)
    v = v_ref[pl.ds(start, wk), :].astype(BF16)
    qpos = i * tq + lax.broadcasted_iota(jnp.int32, (tq, wk), 0)
    kpos = start + lax.broadcasted_iota(jnp.int32, (tq, wk), 1)
    bias1 = jnp.where((kpos <= qpos) & (kpos > qpos - WINDOW), 0.0, NEG)
    s = _dot_nt(q, k) * scale + jnp.concatenate([bias1] * heads, axis=0)
    e = jnp.exp(s - jnp.max(s, axis=1, keepdims=True))
    l = jnp.sum(e, axis=1, keepdims=True)
    o_w = _dot(e.astype(BF16), v) / jnp.maximum(l, 1e-30)
    gates = gate_ref[...]
    for a in range(heads):
        sl = slice(a * HEAD_DIM, (a + 1) * HEAD_DIM)
        o_ref[:, sl] = (gates[:, 3 * a:3 * a + 1] * oc_ref[:, sl]
                        + gates[:, 3 * a + 1:3 * a + 2] * os_ref[:, sl]
                        + gates[:, 3 * a + 2:3 * a + 3] * o_w[a * tq:(a + 1) * tq, :])


def _nsa_prompt(q_n, kv_n, gates, kc_vc, *, tq=128):
    T = q_n.shape[0]
    G = kc_vc.shape[1]
    gw = q_n.shape[1] // G
    n_chunks = kc_vc.shape[2]
    nc = n_chunks - (N_CMP_PARTS - 1)
    ns = T // SEL_BLOCK
    nsp = -(-ns // LANES) * LANES
    tq = min(tq, T)
    nq = T // tq
    q_spec = pl.BlockSpec((tq, gw), lambda g, i: (i, g))
    o_shape = jax.ShapeDtypeStruct((T, G * gw), F32)

    def kv_spec(branch, a):
        return pl.BlockSpec((T, HEAD_DIM), lambda g, i: (0, (branch * 2 + a) * G + g))

    o_c, sel = pl.pallas_call(
        functools.partial(_nsa_cmp_kernel, tq=tq, nc=nc, ns=ns),
        out_shape=(o_shape, jax.ShapeDtypeStruct((G, T, nsp), BF16)),
        grid=(nq,),
        in_specs=[pl.BlockSpec((tq, G * gw), lambda i: (i, 0)),
                  pl.BlockSpec(kc_vc.shape, lambda i: (0, 0, 0, 0))],
        out_specs=(pl.BlockSpec((tq, G * gw), lambda i: (i, 0)), pl.BlockSpec((G, tq, nsp), lambda i: (0, i, 0))),
        compiler_params=_cparams(1),
        name="nsa_cmp",
    )(q_n, kc_vc)
    tk = min(SEL_TILE, T)
    assert T % tk == 0 and tk % SEL_BLOCK == 0
    o_s = pl.pallas_call(
        functools.partial(_nsa_sel_kernel, tq=tq, tk=tk),
        out_shape=o_shape,
        grid=(G, nq),
        in_specs=[q_spec, kv_spec(1, 0), kv_spec(1, 1), pl.BlockSpec((None, tq, nsp), lambda g, i: (g, i, 0))],
        out_specs=q_spec,
        compiler_params=_cparams(2),
        name="nsa_sel",
    )(q_n, kv_n, kv_n, sel)
    wk = min(WINDOW + tq, T)
    assert (T - wk) % tq == 0 and WINDOW % tq == 0
    return pl.pallas_call(
        functools.partial(_nsa_win_kernel, tq=tq, wk=wk),
        out_shape=o_shape,
        grid=(G, nq),
        in_specs=[q_spec, kv_spec(2, 0), kv_spec(2, 1), q_spec, q_spec,
                  pl.BlockSpec((tq, LANES), lambda g, i: (i, g))],
        out_specs=q_spec,
        compiler_params=_cparams(2),
        name="nsa_win",
    )(q_n, kv_n, kv_n, o_c, o_s, gates)


def _mem_kernel(q_ref, kv_ref, o_ref):
    scale = HEAD_DIM ** -0.5
    for h in range(MEM_H):
        sl = slice(h * HEAD_DIM, (h + 1) * HEAD_DIM)
        k = kv_ref[:, sl].astype(BF16)
        v = kv_ref[:, MEM_H * HEAD_DIM + h * HEAD_DIM:MEM_H * HEAD_DIM + (h + 1) * HEAD_DIM].astype(BF16)
        s = _dot_nt(q_ref[:, sl].astype(BF16), k) * scale
        e = jnp.exp(s - jnp.max(s, axis=1, keepdims=True))
        p = e / jnp.sum(e, axis=1, keepdims=True)
        o_ref[:, sl] = _dot(p.astype(BF16), v)


def _mem_attention(qz, kv, *, tq=512):
    T = qz.shape[0]
    M = kv.shape[0]
    E = MEM_H * HEAD_DIM
    tq = min(tq, T)
    return pl.pallas_call(
        _mem_kernel,
        out_shape=jax.ShapeDtypeStruct((T, E), F32),
        grid=(T // tq,),
        in_specs=[pl.BlockSpec((tq, E), lambda i: (i, 0)), pl.BlockSpec((M, 2 * E), lambda i: (0, 0))],
        out_specs=pl.BlockSpec((tq, E), lambda i: (i, 0)),
        compiler_params=_cparams(1),
        name="mem_attention",
    )(qz, kv)


ROWS_PAD = 16


def _head_of_lane(rows, hk):
    lane_head = lax.broadcasted_iota(jnp.int32, (rows, hk * HEAD_DIM), 1) // HEAD_DIM
    row = lax.broadcasted_iota(jnp.int32, (rows, hk * HEAD_DIM), 0)
    return lane_head, row


def _block_diag(x, hk, group):
    lane_head, row = _head_of_lane(x.shape[0], hk)
    return jnp.where(lane_head == row // group, jnp.concatenate([x] * hk, axis=1), 0.0)


def _diag_part(acc, hk, group):
    rows = acc.shape[0]
    row = lax.broadcasted_iota(jnp.int32, (rows, HEAD_DIM), 0)
    out = jnp.zeros((rows, HEAD_DIM), F32)
    for h in range(hk):
        out = jnp.where(row // group == h, acc[:, h * HEAD_DIM:(h + 1) * HEAD_DIM], out)
    return out


def _expand_rows(x, group, rows):
    parts = [jnp.broadcast_to(x[h:h + 1, :], (group, HEAD_DIM)) for h in range(x.shape[0])]
    pad = rows - group * x.shape[0]
    if pad:
        parts.append(jnp.zeros((pad, HEAD_DIM), F32))
    return jnp.concatenate(parts, axis=0) if len(parts) > 1 else parts[0]


def _gather_cat(ref, first, hk, n_keys, stride):
    return jnp.concatenate([ref[pl.ds(first + h, n_keys, stride=stride), :] for h in range(hk)],
                           axis=1).astype(BF16)


def _rep(x, hk):
    return jnp.concatenate([x] * hk, axis=1)


def _suffix_matrix(n):
    r = lax.broadcasted_iota(jnp.int32, (n, 2 * n), 0)
    c = lax.broadcasted_iota(jnp.int32, (n, 2 * n), 1)
    return jnp.where((r > c) | (c >= n), 1.0, 0.0).astype(BF16)


def _lane_rep(col, width=LANES):
    return jnp.broadcast_to(col, (col.shape[0], width))


SUBLANES = 8


def _own_head(rows, n_cols, hk):
    col = lax.broadcasted_iota(jnp.int32, (rows, n_cols), 1)
    row = lax.broadcasted_iota(jnp.int32, (rows, n_cols), 0)
    return ((col & (SUBLANES - 1)) == (row & (SUBLANES - 1))) & (row < hk)


def _page_scores(q, kv_ref, hk):
    qb = q.astype(BF16)
    row = lax.broadcasted_iota(jnp.int32, (ROWS_PAD, PAGE_SIZE * SUBLANES), 0)
    s = None
    for t in range(hk // SUBLANES):
        kt = kv_ref[:, t].reshape(PAGE_SIZE * SUBLANES, HEAD_DIM).astype(BF16)
        st = _dot_nt(qb, kt)
        s = st if s is None else jnp.where(row < t * SUBLANES, s, st)
    return s


def _page_values(w, kv_ref, hk):
    wb = w.astype(BF16)
    row = lax.broadcasted_iota(jnp.int32, (ROWS_PAD, HEAD_DIM), 0)
    o = None
    n_t = hk // SUBLANES
    for t in range(n_t):
        vt = kv_ref[:, n_t + t].reshape(PAGE_SIZE * SUBLANES, HEAD_DIM).astype(BF16)
        ot = _dot(wb, vt)
        o = ot if o is None else jnp.where(row < t * SUBLANES, o, ot)
    return o


def _fox_decode_kernel(pt_ref, gate_ref, q_ref, kvn_ref, lfn_ref, *rest, hk, pps):
    kv_refs, lf_refs = rest[:pps], rest[pps:2 * pps]
    u_ref, o_ref, m_ref, l_ref, c_ref, acc_ref = rest[2 * pps:]
    p = pl.program_id(1)
    scale = HEAD_DIM ** -0.5
    n_cols = PAGE_SIZE * SUBLANES
    q = q_ref[...]

    @pl.when(p == 0)
    def _():
        s_new = jnp.sum(q * kvn_ref[0:hk, :], axis=1, keepdims=True) * scale
        m_ref[...] = _lane_rep(s_new)
        l_ref[...] = jnp.ones_like(l_ref)
        c_ref[...] = lfn_ref[...]
        acc_ref[...] = kvn_ref[hk:2 * hk, :]

    keep = _own_head(ROWS_PAD, n_cols, hk)
    c = c_ref[...]
    scores = []
    for kv_ref, lf_ref in zip(kv_refs, lf_refs):
        lf = lf_ref[...]
        bias = _dot_exact01(lf, u_ref[...]) + _rep(c, n_cols // LANES)
        scores.append(jnp.where(keep, _page_scores(q, kv_ref, hk) * scale + bias, NEG))
        c = c + jnp.sum(lf, axis=1, keepdims=True)
    m_old = m_ref[...]
    m_new = m_old
    for s in scores:
        m_new = jnp.maximum(m_new, jnp.max(s, axis=1, keepdims=True))
    alpha = jnp.exp(m_old - m_new)
    l = alpha * l_ref[...]
    acc = alpha * acc_ref[...]
    for s, kv_ref in zip(scores, kv_refs):
        e = jnp.exp(s - m_new[:, :1])
        l = l + jnp.sum(e, axis=1, keepdims=True)
        acc = acc + _page_values(e, kv_ref, hk)
    l_ref[...] = l
    acc_ref[...] = acc
    m_ref[...] = m_new
    c_ref[...] = c

    @pl.when(p == pl.num_programs(1) - 1)
    def _():
        o_ref[...] = acc / jnp.maximum(l, 1e-30)


def _sb_decode_kernel(pt_ref, gate_ref, q_ref, c0_ref, acc0_ref, *rest, hk, pps):
    kv_refs = rest[:pps]
    o_ref, cout_ref, alive_ref, c_ref, acc_ref = rest[pps:]
    p = pl.program_id(1)
    scale = HEAD_DIM ** -0.5
    n_cols = PAGE_SIZE * SUBLANES

    @pl.when(p == 0)
    def _():
        c_ref[...] = c0_ref[...]
        acc_ref[...] = acc0_ref[...]

    for kv_ref in kv_refs:
        @pl.when(jnp.max(c_ref[...]) > SB_DEAD)
        def _(kv_ref=kv_ref):
            z = _page_scores(q_ref[...], kv_ref, hk) * scale
            keep = _own_head(ROWS_PAD, n_cols, hk)
            sp = _softplus(z)
            log_keep = jnp.where(keep, -sp, 0.0)
            r = lax.broadcasted_iota(jnp.int32, (LANES, LANES), 0)
            cc = lax.broadcasted_iota(jnp.int32, (LANES, LANES), 1)
            suffix = jnp.where(r > cc, 1.0, 0.0).astype(BF16)
            carry = c_ref[...]
            after = [None] * (n_cols // LANES)
            for t in reversed(range(n_cols // LANES)):
                chunk = log_keep[:, t * LANES:(t + 1) * LANES]
                after[t] = _dot_exact01(chunk, suffix, parts=2) + carry
                carry = carry + jnp.sum(chunk, axis=1, keepdims=True)
            a = jnp.where(keep, jnp.exp((z - sp) + jnp.concatenate(after, axis=1)), 0.0)
            acc_ref[...] = acc_ref[...] + _page_values(a, kv_ref, hk)
            c_ref[...] = carry

    @pl.when(p == pl.num_programs(1) - 1)
    def _():
        o_ref[...] = acc_ref[...]
        cout_ref[...] = c_ref[...]
        alive_ref[...] = jnp.where(jnp.max(c_ref[...], axis=0, keepdims=True) > SB_DEAD, 1.0, 0.0) + jnp.zeros_like(c_ref)


def _paged_decode(kern, pools, page_table, layer, small_inputs, const_inputs, scratch, *, pps, reverse, name,
                  first=0, count=None, gate=None, n_out=1):
    B, n_pages = page_table.shape
    count = n_pages - first if count is None else count
    assert count % pps == 0 and first + count <= n_pages
    pt = page_table.reshape(B * n_pages)
    gate = jnp.ones((B,), jnp.int32) if gate is None else gate

    def page_spec(arr, n_pool, j):
        def index(b, p, pt_ref, gate_ref):
            pp = first + p * pps + j
            if reverse:
                pp = n_pages - 1 - pp
            page = jnp.where(gate_ref[b] > 0, pt_ref[b * n_pages + pp], 0)
            return (layer * n_pool + page,) + (0,) * (arr.ndim - 1)
        return pl.BlockSpec((None,) + arr.shape[1:], index)

    row_spec = pl.BlockSpec((None, ROWS_PAD, HEAD_DIM), lambda b, p, pt_ref, gate_ref: (b, 0, 0))
    in_specs = [pl.BlockSpec((None,) + x.shape[1:], lambda b, p, pt_ref, gate_ref: (b, 0, 0)) for x in small_inputs]
    args = list(small_inputs)
    for arr, n_pool in pools:
        for j in range(pps):
            in_specs.append(page_spec(arr, n_pool, j))
            args.append(arr)
    for x in const_inputs:
        in_specs.append(pl.BlockSpec(x.shape, lambda b, p, pt_ref, gate_ref, nd=x.ndim: (0,) * nd))
        args.append(x)
    row_shape = jax.ShapeDtypeStruct((B, ROWS_PAD, HEAD_DIM), F32)
    return pl.pallas_call(
        kern,
        out_shape=row_shape if n_out == 1 else (row_shape,) * n_out,
        grid_spec=pltpu.PrefetchScalarGridSpec(
            num_scalar_prefetch=2,
            grid=(B, count // pps),
            in_specs=in_specs,
            out_specs=row_spec if n_out == 1 else (row_spec,) * n_out,
            scratch_shapes=scratch),
        compiler_params=_cparams(2),
        name=name,
    )(pt, gate, *args)


def _pad_rows(x, rows=ROWS_PAD):
    return jnp.pad(x, ((0, 0), (0, rows - x.shape[1]), (0, 0)))


def _head_tiled_pool(pool):
    n_layers, n_pool, _, _, hk, _ = pool.shape
    assert hk % SUBLANES == 0
    return pool.reshape(n_layers * n_pool, PAGE_SIZE, 2 * hk // SUBLANES, SUBLANES, HEAD_DIM)


def _fox_decode(q, kv_new, logf_new, pool, logf_pool, page_table, layer, *, pps=4):
    B = q.shape[0]
    hk = pool.shape[4]
    assert hk == ROWS_PAD
    n_pool = pool.shape[1]
    pps = math.gcd(pps, page_table.shape[1])
    lf_t = jnp.swapaxes(logf_pool, 2, 3).reshape(logf_pool.shape[0] * n_pool, hk, PAGE_SIZE)
    small = [q.reshape(B, hk, HEAD_DIM), kv_new.reshape(B, 2 * hk, HEAD_DIM),
             jnp.broadcast_to(logf_new[:, :, None], (B, hk, LANES))]
    n_cols = PAGE_SIZE * SUBLANES
    key_of_col = lax.broadcasted_iota(jnp.int32, (PAGE_SIZE, n_cols), 1) // SUBLANES
    u = (lax.broadcasted_iota(jnp.int32, (PAGE_SIZE, n_cols), 0) > key_of_col).astype(BF16)
    scratch = [pltpu.VMEM((ROWS_PAD, LANES), F32)] * 4
    o = _paged_decode(functools.partial(_fox_decode_kernel, hk=hk, pps=pps),
                      [(_head_tiled_pool(pool), n_pool), (lf_t, n_pool)], page_table, layer, small, [u], scratch,
                      pps=pps, reverse=True, name="fox_decode")
    return o.reshape(B, hk * HEAD_DIM)


def _sb_decode(q, pool, page_table, layer, *, pps=4):
    B, n_pages = page_table.shape
    hk = pool.shape[4]
    pps = math.gcd(pps, n_pages)
    q3 = _pad_rows(q.reshape(B, hk, HEAD_DIM))
    row = lax.broadcasted_iota(jnp.int32, (B, ROWS_PAD, LANES), 1)
    carry = jnp.where(row < hk, 0.0, NEG)
    acc = jnp.zeros((B, ROWS_PAD, HEAD_DIM), F32)
    gate = None
    pools = [(_head_tiled_pool(pool), pool.shape[1])]
    scratch = [pltpu.VMEM((ROWS_PAD, LANES), F32)] * 2
    for first, count in ((0, pps), (pps, n_pages - pps)):
        if count:
            acc, carry, alive = _paged_decode(
                functools.partial(_sb_decode_kernel, hk=hk, pps=pps), pools, page_table, layer, [q3, carry, acc], [],
                scratch, pps=pps, reverse=True, name="sb_decode", first=first, count=count, gate=gate, n_out=3)
            gate = alive[:, 0, 0].astype(jnp.int32)
    return acc[:, :hk].reshape(B, hk * HEAD_DIM)


def _cmp_decode_kernel(pt_ref, *refs, G, nb, n_pages):
    page_refs = refs[:n_pages]
    pe_ref, w1_ref, w2_ref, o_ref, x_ref = refs[n_pages:]
    s = pl.program_id(1)
    cpp = PAGE_SIZE // CMP_STRIDE
    slots = 2 * G
    for p, pool_ref in enumerate(page_refs):
        base = pl.multiple_of((s * n_pages + p) * cpp, cpp)
        for a in range(2):
            for g in range(G):
                for j in range(CMP_STRIDE):
                    x_ref[a * G + g, pl.ds(base, cpp), j * HEAD_DIM:(j + 1) * HEAD_DIM] = (
                        pool_ref[pl.ds(j * slots + a * G + g, cpp, stride=CMP_STRIDE * slots), :])

    @pl.when(s == nb - 1)
    def _():
        for a in range(2):
            for g in range(G):
                x = x_ref[a * G + g]
                sums = [_dot((x + pe_ref[a, part:part + 1, :]).astype(BF16), w1_ref[a, g, part])
                        for part in range(N_CMP_PARTS)]
                _compress_mlp(sums, w2_ref.at[a, g], o_ref.at[a, g])


def _cmp_decode(pool, page_table, layer, pe, w1, w2, *, nb=4):
    n_pool = pool.shape[1]
    G = pool.shape[4]
    slots = 2 * G
    B, n_pages = page_table.shape
    nb = min(nb, B)
    assert B % nb == 0
    cpp = PAGE_SIZE // CMP_STRIDE
    rows = nb * n_pages * cpp
    pool2 = pool.reshape(pool.shape[0] * n_pool, PAGE_SIZE * slots, HEAD_DIM)
    pt = page_table.reshape(B * n_pages)
    pe2 = pe.reshape(2, N_CMP_PARTS, CMP_STRIDE * HEAD_DIM)
    w1r = w1.reshape(2, G, N_CMP_PARTS, CMP_STRIDE * HEAD_DIM, HEAD_DIM)

    def page_spec(p):
        return pl.BlockSpec((None, PAGE_SIZE * slots, HEAD_DIM),
                            lambda bb, s, pt_ref: (layer * n_pool + pt_ref[(bb * nb + s) * n_pages + p], 0, 0))

    return pl.pallas_call(
        functools.partial(_cmp_decode_kernel, G=G, nb=nb, n_pages=n_pages),
        out_shape=jax.ShapeDtypeStruct((2, G, B * n_pages * cpp, HEAD_DIM), F32),
        grid_spec=pltpu.PrefetchScalarGridSpec(
            num_scalar_prefetch=1,
            grid=(B // nb, nb),
            in_specs=[page_spec(p) for p in range(n_pages)] + [
                pl.BlockSpec(pe2.shape, lambda bb, s, pt_ref: (0, 0, 0)),
                pl.BlockSpec(w1r.shape, lambda bb, s, pt_ref: (0, 0, 0, 0, 0)),
                pl.BlockSpec(w2.shape, lambda bb, s, pt_ref: (0, 0, 0, 0))],
            out_specs=pl.BlockSpec((2, G, rows, HEAD_DIM), lambda bb, s, pt_ref: (0, 0, bb, 0)),
            scratch_shapes=[pltpu.VMEM((2 * G, rows, CMP_STRIDE * HEAD_DIM), F32)]),
        compiler_params=_cparams(2),
        name="cmp_decode",
    )(pt, *([pool2] * n_pages), pe2, w1r, w2)


def _nsa_cmp_decode_kernel(q_ref, kcvc_ref, oc_ref, sel_ref, *, G, qpos, nc, ns, nseq):
    scale = HEAD_DIM ** -0.5
    group = NSA_GROUP
    ncp = kcvc_ref.shape[2] // nseq
    nsp = sel_ref.shape[2]
    pooling = _cmp_to_sel_matrix(ncp, nsp, nc)
    imp_sel = []
    for i in range(nseq):
        q = q_ref[i]
        rows = slice(i * ncp, (i + 1) * ncp)
        kcat = jnp.concatenate([kcvc_ref[0, g, rows, :] for g in range(G)], axis=1).astype(BF16)
        vcat = jnp.concatenate([kcvc_ref[1, g, rows, :] for g in range(G)], axis=1).astype(BF16)
        s = _dot_nt(_block_diag(q, G, group).astype(BF16), kcat) * scale
        cidx = lax.broadcasted_iota(jnp.int32, s.shape, 1)
        valid = (cidx * CMP_STRIDE + (CMP_BLOCK - 1) <= qpos) & (cidx < nc)
        s = jnp.where(valid, s, NEG)
        e = jnp.where(valid, jnp.exp(s - jnp.max(s, axis=1, keepdims=True)), 0.0)
        p = e / jnp.maximum(jnp.sum(e, axis=1, keepdims=True), 1e-30)
        oc_ref[i] = _diag_part(_dot(p.astype(BF16), vcat), G, group)
        imp = _expand_rows(jnp.concatenate(
            [jnp.sum(p[g * group:(g + 1) * group], axis=0, keepdims=True) for g in range(G)], axis=0), group, ROWS_PAD)
        imp_sel.append(_dot_exact01(imp, pooling))
    imp_all = jnp.concatenate(imp_sel, axis=0) if nseq > 1 else imp_sel[0]
    sel = _select_blocks(imp_all, jnp.full(imp_all.shape, qpos, jnp.int32), ns, nsp).astype(BF16)
    for i in range(nseq):
        sel_ref[i] = sel[i * ROWS_PAD:(i + 1) * ROWS_PAD]


def _online_tile(s, valid, vcat, m_old, l_old, acc_old, hk):
    s = jnp.where(valid, s, NEG)
    m_new = jnp.maximum(m_old, jnp.max(s, axis=1, keepdims=True))
    e = jnp.where(valid, jnp.exp(s - m_new[:, :1]), 0.0)
    alpha = jnp.exp(m_old - m_new)
    l_new = alpha * l_old + jnp.sum(e, axis=1, keepdims=True)
    acc_new = _rep(alpha, hk) * acc_old + _dot(e.astype(BF16), vcat)
    return m_new, l_new, acc_new


def _nsa_sel_decode_kernel(pt_ref, gate_ref, q_ref, kvn_ref, sel_ref, *refs, G):
    page_refs, o_ref = refs[:-1], refs[-1]
    scale = HEAD_DIM ** -0.5
    group = NSA_GROUP
    slots = 2 * G
    n_keys = len(page_refs) * PAGE_SIZE
    q = q_ref[...]
    k_new = _expand_rows(kvn_ref[slots:slots + G, :], group, ROWS_PAD)
    v_new = _expand_rows(kvn_ref[slots + G:2 * slots, :], group, ROWS_PAD)
    m0 = _lane_rep(jnp.sum(q * k_new, axis=1, keepdims=True) * scale)
    kcat = jnp.concatenate([_gather_cat(r, 0, G, PAGE_SIZE, slots) for r in page_refs], axis=0)
    vcat = jnp.concatenate([_gather_cat(r, G, G, PAGE_SIZE, slots) for r in page_refs], axis=0)
    s = _dot_nt(_block_diag(q, G, group).astype(BF16), kcat) * scale
    nsp = sel_ref.shape[1]
    jidx = lax.broadcasted_iota(jnp.int32, (nsp, n_keys), 0)
    kblk = lax.broadcasted_iota(jnp.int32, (nsp, n_keys), 1) // SEL_BLOCK
    valid = _dot(sel_ref[...], jnp.where(jidx == kblk, 1.0, 0.0).astype(BF16)) > 0.5
    _, l, acc = _online_tile(s, valid, vcat, m0, jnp.ones_like(m0), _block_diag(v_new, G, group), G)
    o_ref[...] = _diag_part(acc, G, group) / jnp.maximum(l, 1e-30)


def _nsa_win_decode_kernel(q_ref, kvn_ref, win_ref, oc_ref, os_ref, gate_ref, o_ref, wnext_ref, *, G, wb, past_len):
    scale = HEAD_DIM ** -0.5
    group = NSA_GROUP
    slots = 2 * G
    q = q_ref[...]
    wnext_ref[0:(wb - 1) * slots, :] = win_ref[slots:wb * slots, :]
    wnext_ref[(wb - 1) * slots:wb * slots, :] = kvn_ref[2 * slots:3 * slots, :]
    k_new = _expand_rows(kvn_ref[2 * slots:2 * slots + G, :], group, ROWS_PAD)
    v_new = _expand_rows(kvn_ref[2 * slots + G:3 * slots, :], group, ROWS_PAD)
    m0 = _lane_rep(jnp.sum(q * k_new, axis=1, keepdims=True) * scale)
    kcat = _gather_cat(win_ref, 0, G, wb, slots)
    vcat = _gather_cat(win_ref, G, G, wb, slots)
    s = _dot_nt(_block_diag(q, G, group).astype(BF16), kcat) * scale
    wpos = past_len - wb + lax.broadcasted_iota(jnp.int32, s.shape, 1)
    valid = (wpos <= past_len) & (wpos > past_len - WINDOW) & (wpos >= 0)
    _, l, acc = _online_tile(s, valid, vcat, m0, jnp.ones_like(m0), _block_diag(v_new, G, group), G)
    o_w = _diag_part(acc, G, group) / jnp.maximum(l, 1e-30)
    gates = gate_ref[...]
    o_ref[...] = gates[:, 0:1] * oc_ref[...] + gates[:, 1:2] * os_ref[...] + gates[:, 2:3] * o_w


def _nsa_decode(q_n, kv_new, gates, cmp_pool, sel_pool, win_buf, page_table, layer, pe, w1, w2):
    B, n_pages = page_table.shape
    G = cmp_pool.shape[4]
    heads = G * NSA_GROUP
    slots = 2 * G
    past_len = n_pages * PAGE_SIZE
    assert past_len % CMP_STRIDE == 0
    q3 = _pad_rows(q_n.reshape(B, heads, HEAD_DIM))
    kvn3 = _pad_rows(kv_new.reshape(B, 3 * slots, HEAD_DIM))
    g3 = _pad_rows(jnp.pad(gates.reshape(B, heads, 3), ((0, 0), (0, 0), (0, LANES - 3))))
    cpp = PAGE_SIZE // CMP_STRIDE
    ncp = n_pages * cpp
    l_pad = -(-(past_len + 1) // SEL_BLOCK) * SEL_BLOCK
    ns = l_pad // SEL_BLOCK
    nsp = -(-ns // LANES) * LANES
    row_spec = pl.BlockSpec((None, ROWS_PAD, HEAD_DIM), lambda b: (b, 0, 0))
    row_shape = jax.ShapeDtypeStruct((B, ROWS_PAD, HEAD_DIM), F32)

    kcvc = _cmp_decode(cmp_pool, page_table, layer, pe, w1, w2)
    nseq = math.gcd(B, 8)
    seq_spec = pl.BlockSpec((nseq, ROWS_PAD, HEAD_DIM), lambda b: (b, 0, 0))
    o_c, sel = pl.pallas_call(
        functools.partial(_nsa_cmp_decode_kernel, G=G, qpos=past_len, nc=ncp, ns=ns, nseq=nseq),
        out_shape=(row_shape, jax.ShapeDtypeStruct((B, ROWS_PAD, nsp), BF16)),
        grid=(B // nseq,),
        in_specs=[seq_spec, pl.BlockSpec((2, G, nseq * ncp, HEAD_DIM), lambda b: (0, 0, b, 0))],
        out_specs=(seq_spec, pl.BlockSpec((nseq, ROWS_PAD, nsp), lambda b: (b, 0, 0))),
        compiler_params=_cparams(1),
        name="nsa_cmp_decode",
    )(q3, kcvc)
    n_pool = sel_pool.shape[1]
    sel_pool2 = sel_pool.reshape(sel_pool.shape[0] * n_pool, PAGE_SIZE * slots, HEAD_DIM)
    o_s = _paged_decode(functools.partial(_nsa_sel_decode_kernel, G=G), [(sel_pool2, n_pool)], page_table, layer,
                        [q3, kvn3, sel], [], [], pps=n_pages, reverse=False, name="nsa_sel_decode")
    wb = win_buf.shape[2]
    win2 = win_buf.reshape(win_buf.shape[0] * B, wb * slots, HEAD_DIM)
    o_n, w_next = pl.pallas_call(
        functools.partial(_nsa_win_decode_kernel, G=G, wb=wb, past_len=past_len),
        out_shape=(row_shape, jax.ShapeDtypeStruct((B, wb * slots, HEAD_DIM), F32)),
        grid=(B,),
        in_specs=[row_spec, row_spec,
                  pl.BlockSpec((None, wb * slots, HEAD_DIM), lambda b: (layer * B + b, 0, 0)),
                  row_spec, row_spec, row_spec],
        out_specs=(row_spec, pl.BlockSpec((None, wb * slots, HEAD_DIM), lambda b: (b, 0, 0))),
        compiler_params=_cparams(1),
        name="nsa_win_decode",
    )(q3, kvn3, win2, o_c, o_s, g3)
    return o_n[:, :heads].reshape(B, heads * HEAD_DIM), w_next.reshape(B, wb, 2, G, HEAD_DIM)


def _mem_decode_kernel(q_ref, kv_ref, o_ref, *, n_mem):
    scale = HEAD_DIM ** -0.5
    slots = 2 * MEM_H
    kcat = _gather_cat(kv_ref, 0, MEM_H, n_mem, slots)
    vcat = _gather_cat(kv_ref, MEM_H, MEM_H, n_mem, slots)
    s = _dot_nt(_block_diag(q_ref[...], MEM_H, 1).astype(BF16), kcat) * scale
    e = jnp.exp(s - jnp.max(s, axis=1, keepdims=True))
    p = e / jnp.sum(e, axis=1, keepdims=True)
    o_ref[...] = _diag_part(_dot(p.astype(BF16), vcat), MEM_H, 1)


def _mem_decode(qz, cache_mem, layer):
    B = qz.shape[0]
    n_mem = cache_mem.shape[2]
    slots = 2 * MEM_H
    E = MEM_H * HEAD_DIM
    q3 = _pad_rows(qz[:, :E].reshape(B, MEM_H, HEAD_DIM))
    kv2 = cache_mem.reshape(cache_mem.shape[0] * B, n_mem * slots, HEAD_DIM)
    row_spec = pl.BlockSpec((None, ROWS_PAD, HEAD_DIM), lambda b: (b, 0, 0))
    o = pl.pallas_call(
        functools.partial(_mem_decode_kernel, n_mem=n_mem),
        out_shape=jax.ShapeDtypeStruct((B, ROWS_PAD, HEAD_DIM), F32),
        grid=(B,),
        in_specs=[row_spec, pl.BlockSpec((None, n_mem * slots, HEAD_DIM), lambda b: (layer * B + b, 0, 0))],
        out_specs=row_spec,
        compiler_params=_cparams(1),
        name="mem_decode",
    )(q3, kv2)
    return o[:, :MEM_H].reshape(B, E)


TM_PROJ = 1024
TN_PROJ = 512
TM_OUT = 512


def _even_weights(w_in, w_out, cmp_w1, cmp_w2):
    G = cmp_w1.shape[1]
    hd = HEAD_DIM
    nq = G * NSA_GROUP * hd
    sizes = (nq, 6 * G * hd, 3 * G * NSA_GROUP, nq)
    c0, c1, c2, c3 = (sum(sizes[:k + 1]) for k in range(4))
    n_rest = w_in.shape[1] - c3
    sb_w = n_rest // 4
    gate = w_in[:, c1:c2].reshape(-1, G, NSA_GROUP * 3)
    gate_grouped = jnp.pad(gate, ((0, 0), (0, 0), (0, LANES - NSA_GROUP * 3))).reshape(-1, G * LANES)
    gate_flat = jnp.pad(w_in[:, c1:c2], ((0, 0), (0, LANES - (c2 - c1))))
    return dict(
        q_n=w_in[:, :c0].astype(BF16),
        kv_n=w_in[:, c0:c1].astype(BF16),
        gate_grouped=gate_grouped.astype(BF16),
        gate_flat=gate_flat.astype(BF16),
        z=jnp.concatenate([w_in[:, c2:c3], w_in[:, c3 + 3 * sb_w:]], axis=1).astype(BF16),
        sb_q=w_in[:, c3:c3 + sb_w].astype(BF16),
        sb_kv=w_in[:, c3 + sb_w:c3 + 3 * sb_w].astype(BF16),
        out=w_out.astype(BF16),
        w1=cmp_w1.astype(BF16),
        w2=cmp_w2.astype(BF16),
        G=G, sb_h=sb_w // hd,
    )


def _odd_weights(w_in, b_f, w_out):
    n_h = b_f.shape[0]
    c1 = 3 * n_h * HEAD_DIM
    return dict(
        q=w_in[:, :c1 // 3].astype(BF16),
        kv=w_in[:, c1 // 3:c1].astype(BF16),
        f=jnp.pad(w_in[:, c1:c1 + n_h], ((0, 0), (0, LANES - n_h))).astype(BF16),
        b_f=jnp.pad(b_f, (0, LANES - n_h)),
        z=w_in[:, c1 + n_h:].astype(BF16),
        out=w_out.astype(BF16),
        n_h=n_h,
    )


def _proj(x, g, w, rope=None, rope_heads=(), **kw):
    tn = TN_PROJ if w.shape[1] % TN_PROJ == 0 else min(w.shape[1], 256)
    return _norm_proj(x, g, w, tm=TM_PROJ, tn=tn, rope=rope, rope_heads=rope_heads, **kw)


def _even_project(x, g, W, rope):
    q_n = _proj(x, g, W["q_n"], rope, (1, 1, 1, 1))
    kv_n = _proj(x, g, W["kv_n"], rope, (1, 1, 0, 0))
    z = _proj(x, g, W["z"])
    sb_q = _proj(x, g, W["sb_q"])
    sb_kv = _proj(x, g, W["sb_kv"])
    return q_n, kv_n, z, sb_q, sb_kv


def _mem_block(xp, xs, i, mem_prompt, cache_mem, mx_norm, mx_mem_norm, mx_w_qz, mx_w_kv, mx_w_o):
    w_qz = mx_w_qz[i].astype(BF16)
    w_o = mx_w_o[i].astype(BF16)
    kv_mem = _proj(mem_prompt, mx_mem_norm[i], mx_w_kv[i].astype(BF16))
    qz = _proj(xp, mx_norm[i], w_qz)
    xp = _gated_out([_mem_attention(qz, kv_mem)], qz, 1, w_o, xp, tm=TM_OUT, tn=TN_PROJ)
    qz = _proj(xs, mx_norm[i], w_qz)
    xs = _gated_out([_mem_decode(qz, cache_mem, i)], qz, 1, w_o, xs, tm=TM_OUT, tn=TN_PROJ)
    return xp, xs, kv_mem


def kernel(x_prompt, x_sample, cache_nsa_cmp, cache_nsa_sel, cache_nsa_win, cache_sb, cache_fox_kv, cache_fox_logf, cache_mem, page_table, mem_prompt, ev_norm, ev_w_in, ev_cmp_pe, ev_cmp_w1, ev_cmp_w2, ev_w_out, od_norm, od_w_in, od_b_f, od_w_out, mx_norm, mx_mem_norm, mx_w_qz, mx_w_kv, mx_w_o, final_norm):
    bp, T, D = x_prompt.shape
    B, t_s, _ = x_sample.shape
    assert bp == 1 and t_s == 1
    depth = mx_norm.shape[0]
    n_pages = page_table.shape[1]
    past_len = n_pages * PAGE_SIZE
    rope_p = _rope_tables(jnp.arange(T, dtype=jnp.int32))
    rope_s = _rope_tables(jnp.full((B,), past_len, jnp.int32))
    xp = x_prompt.reshape(T, D)
    xs = x_sample.reshape(B, D)
    mem = mem_prompt.reshape(mem_prompt.shape[1], D)
    outs = {k: [] for k in ("cmp_p", "cmp_s", "sel_p", "sel_s", "win_p", "win_s", "sb_p", "sb_s",
                            "fkv_p", "fkv_s", "flf_p", "flf_s", "mem_p")}
    for i in range(depth):
        if i % 2 == 0:
            e = i // 2
            W = _even_weights(ev_w_in[e], ev_w_out[e], ev_cmp_w1[e], ev_cmp_w2[e])
            G, sb_h = W["G"], W["sb_h"]
            bw = 2 * G * HEAD_DIM
            w1_chunks = W["w1"].reshape(2, G, CMP_BLOCK, HEAD_DIM, HEAD_DIM)
            q_n, kv_n, z, sb_q, sb_kv = _even_project(xp, ev_norm[e], W, rope_p)
            gates = _proj(xp, ev_norm[e], W["gate_grouped"], act="sigmoid")
            kc_vc = _nsa_compress(kv_n, ev_cmp_pe[e], w1_chunks, W["w2"])
            o_n = _nsa_prompt(q_n, kv_n, gates, kc_vc)
            o_s = _sb_attention(sb_q, sb_kv, sb_h)
            xp = _gated_out([o_n, o_s], z, 0, W["out"], xp, tm=TM_OUT, tn=TN_PROJ)
            wbp = min(WINDOW, T)
            outs["cmp_p"].append(kv_n[:, :bw].reshape(1, T, 2, G, HEAD_DIM))
            outs["sel_p"].append(kv_n[:, bw:2 * bw].reshape(1, T, 2, G, HEAD_DIM))
            outs["win_p"].append(kv_n[T - wbp:, 2 * bw:].reshape(1, wbp, 2, G, HEAD_DIM))
            outs["sb_p"].append(sb_kv.reshape(1, T, 2, sb_h, HEAD_DIM))
            q_n, kv_n, z, sb_q, sb_kv = _even_project(xs, ev_norm[e], W, rope_s)
            gates = _proj(xs, ev_norm[e], W["gate_flat"], act="sigmoid")[:, :3 * G * NSA_GROUP]
            o_n, win_next = _nsa_decode(q_n, kv_n, gates, cache_nsa_cmp, cache_nsa_sel, cache_nsa_win, page_table, e,
                                        ev_cmp_pe[e], W["w1"], W["w2"])
            o_s = _sb_decode(sb_q, cache_sb, page_table, e)
            xs = _gated_out([o_n, o_s], z, 0, W["out"], xs, tm=TM_OUT, tn=TN_PROJ)
            kv5 = kv_n.reshape(B, 1, 3, 2, G, HEAD_DIM)
            outs["cmp_s"].append(kv5[:, :, 0])
            outs["sel_s"].append(kv5[:, :, 1])
            outs["win_s"].append(win_next)
            outs["sb_s"].append(sb_kv.reshape(B, 1, 2, sb_h, HEAD_DIM))
        else:
            o = i // 2
            W = _odd_weights(od_w_in[o], od_b_f[o], od_w_out[o])
            n_h = W["n_h"]
            q = _proj(xp, od_norm[o], W["q"])
            kv = _proj(xp, od_norm[o], W["kv"])
            z = _proj(xp, od_norm[o], W["z"])
            logf = _proj(xp, od_norm[o], W["f"], bias=W["b_f"], act="logsigmoid")
            f_cum = _cumsum_rows(logf)[:, :n_h]
            xp = _gated_out([_fox_attention(q, kv, f_cum, n_h)], z, 0, W["out"], xp, tm=TM_OUT, tn=TN_PROJ)
            outs["fkv_p"].append(kv.reshape(1, T, 2, n_h, HEAD_DIM))
            outs["flf_p"].append(logf[:, :n_h].reshape(1, T, n_h))
            q = _proj(xs, od_norm[o], W["q"])
            kv = _proj(xs, od_norm[o], W["kv"])
            z = _proj(xs, od_norm[o], W["z"])
            logf = _proj(xs, od_norm[o], W["f"], bias=W["b_f"], act="logsigmoid")[:, :n_h]
            att = _fox_decode(q, kv, logf, cache_fox_kv, cache_fox_logf, page_table, o)
            xs = _gated_out([att], z, 0, W["out"], xs, tm=TM_OUT, tn=TN_PROJ)
            outs["fkv_s"].append(kv.reshape(B, 1, 2, n_h, HEAD_DIM))
            outs["flf_s"].append(logf.reshape(B, 1, n_h))
        xp, xs, kv_mem = _mem_block(xp, xs, i, mem, cache_mem, mx_norm, mx_mem_norm, mx_w_qz, mx_w_kv, mx_w_o)
        outs["mem_p"].append(kv_mem.reshape(1, mem.shape[0], 2, MEM_H, HEAD_DIM))
    y_prompt = _rmsnorm(xp, final_norm, tm=TM_OUT).reshape(1, T, D)
    y_sample = _rmsnorm(xs, final_norm, tm=TM_OUT).reshape(B, 1, D)
    return (y_prompt, y_sample) + tuple(
        jnp.stack(outs[k]) for k in ("cmp_p", "cmp_s", "sel_p", "sel_s", "win_p", "win_s", "sb_p", "sb_s",
                                     "fkv_p", "fkv_s", "flf_p", "flf_s", "mem_p"))
```
